```python
import math
import jax
import jax.numpy as jnp
from jax import lax
import numpy as np

D_MODEL = 1024
BATCH = 16
SEQ = 2048
DEPTH = 4
DEC_BATCH = 16
DEC_SEQ = 4096
PAST_LEN = 128

F32 = jnp.float32
NORM_EPS = 1e-6
HEAD_DIM = 64
ROT_DIM = HEAD_DIM // 4
ROPE_THETA = 500000.0

A_PATTERNS = ((128, 1), (512, 4), (2048, 16))
A_GROUPS = len(A_PATTERNS)
A_HEADS = 6
A_WIDTH = A_HEADS * HEAD_DIM
A_QKV = A_GROUPS * A_WIDTH

B_HEADS = 4
B_HEAD_DIM = 128
B_WIDTH = B_HEADS * B_HEAD_DIM
B_CONV = 3
B_CHUNK = 64

C_HEADS = 4
C_QK_DIM = HEAD_DIM
C_V_DIM = 2 * C_QK_DIM
C_QK_WIDTH = C_HEADS * 2 * C_QK_DIM
C_WIDTH = C_HEADS * C_V_DIM
C_QBLK = 128

N_BRANCH = 3
BRANCH_WIDTH = A_WIDTH + B_WIDTH + C_WIDTH
D_FF = -(-8 * D_MODEL // (3 * 256)) * 256

IN_SPLITS = (A_QKV, A_QKV, A_QKV, B_WIDTH, B_WIDTH, B_WIDTH, 4 * B_HEADS,
             C_QK_WIDTH, C_QK_WIDTH, C_WIDTH, N_BRANCH * D_MODEL)
IN_COLS = sum(IN_SPLITS)
IN_OFFSETS = tuple(int(v) for v in np.cumsum(IN_SPLITS)[:-1])

kernel_name = "hybrid_dilated_mlstm_diffattn_encoder"


def rms_norm(x, g):
    xf = x.astype(F32)
    y = xf * lax.rsqrt(jnp.mean(xf * xf, axis=-1, keepdims=True) + NORM_EPS)
    return (y * g.astype(F32)).astype(x.dtype)


def rotary_tables(seq, dtype):
    inv = ROPE_THETA ** (-jnp.arange(0, ROT_DIM, 2, dtype=F32) / ROT_DIM)
    ang = jnp.arange(seq, dtype=F32)[:, None] * inv[None, :]
    return jnp.cos(ang).astype(dtype), jnp.sin(ang).astype(dtype)


def partial_rotary(x, cos, sin):
    extra = x.ndim - 3
    c = cos.reshape(cos.shape[0], *([1] * extra), cos.shape[1])
    s = sin.reshape(sin.shape[0], *([1] * extra), sin.shape[1])
    half = ROT_DIM // 2
    x1, x2, xp = x[..., :half], x[..., half:ROT_DIM], x[..., ROT_DIM:]
    return jnp.concatenate([x1 * c - x2 * s, x2 * c + x1 * s, xp], axis=-1)


def dilated_window_attention(q, k, v, dilation, half):
    Bn, S, H, Dh = q.shape
    L = S // dilation
    nb = -(-L // half)
    Lp = nb * half

    def to_strided(t):
        return t.reshape(Bn, L, dilation, H, Dh).transpose(0, 2, 3, 1, 4)

    qs, ks, vs = to_strided(q), to_strided(k), to_strided(v)
    qb = jnp.pad(qs, ((0, 0), (0, 0), (0, 0), (0, Lp - L), (0, 0))).reshape(Bn, dilation, H, nb, half, Dh)
    pad_kv = ((0, 0), (0, 0), (0, 0), (half, Lp - L + half), (0, 0))
    kp, vp = jnp.pad(ks, pad_kv), jnp.pad(vs, pad_kv)

    def windows(t):
        return jnp.concatenate(
            [t[:, :, :, j * half:j * half + Lp].reshape(Bn, dilation, H, nb, half, Dh) for j in range(3)], axis=4)

    kw, vw = windows(kp), windows(vp)
    s = jnp.einsum('brhnqd,brhnkd->brhnqk', qb, kw).astype(F32) * (1.0 / math.sqrt(Dh))
    qpos = jnp.arange(nb)[:, None, None] * half + jnp.arange(half)[None, :, None]
    kpos = jnp.arange(nb)[:, None, None] * half + jnp.arange(3 * half)[None, None, :] - half
    mask = (kpos >= 0) & (kpos < L) & (jnp.abs(qpos - kpos) <= half)
    s = jnp.where(mask, s, -1e30)
    mx = jnp.max(s, axis=-1, keepdims=True)
    p = jnp.exp(s - mx)
    den = jnp.sum(p, axis=-1, keepdims=True)
    o = jnp.einsum('brhnqk,brhnkd->brhnqd', (p / den).astype(v.dtype), vw)
    lse = (mx + jnp.log(den))[..., 0]
    o = o.reshape(Bn, dilation, H, Lp, Dh)[:, :, :, :L].transpose(0, 3, 1, 2, 4).reshape(Bn, S, H, Dh)
    lse = lse.reshape(Bn, dilation, H, Lp)[:, :, :, :L].transpose(0, 3, 1, 2).reshape(Bn, S, H)
    return o, lse


def mlstm_chunkwise(q, k, v, log_i, log_f):
    Bn, H, S, Dh = q.shape
    L = B_CHUNK
    nc = S // L

    def chunks(t):
        return jnp.moveaxis(t.astype(F32).reshape(Bn, H, nc, L, *t.shape[3:]), 2, 0)

    xs = (chunks(q), chunks(k), chunks(v), chunks(log_i), chunks(log_f))
    tri = jnp.tril(jnp.ones((L, L), dtype=bool))

    def step(carry, inp):
        C, n, m = carry
        qt, kt, vt, li, lf = inp
        b = jnp.cumsum(lf, axis=-1)
        b_last = b[..., -1]
        logD = jnp.where(tri, b[..., :, None] - b[..., None, :] + li[..., None, :], -jnp.inf)
        inter = b + m[..., None]
        m_t = jnp.maximum(inter, jnp.max(logD, axis=-1))
        sc = jnp.einsum('bhtd,bhsd->bhts', qt, kt) * jnp.exp(logD - m_t[..., None])
        w_inter = jnp.exp(inter - m_t)
        num = w_inter[..., None] * jnp.einsum('bhtd,bhde->bhte', qt, C) + jnp.einsum('bhts,bhse->bhte', sc, vt)
        den = w_inter * jnp.einsum('bhtd,bhd->bht', qt, n) + jnp.sum(sc, axis=-1)
        h = num / jnp.maximum(jnp.abs(den), jnp.exp(-m_t))[..., None]
        g = b_last[..., None] - b + li
        m_new = jnp.maximum(b_last + m, jnp.max(g, axis=-1))
        decay = jnp.exp(b_last + m - m_new)
        wk = jnp.exp(g - m_new[..., None])[..., None] * kt
        C_new = decay[..., None, None] * C + jnp.einsum('bhsd,bhse->bhde', wk, vt)
        n_new = decay[..., None] * n + jnp.sum(wk, axis=2)
        return (C_new, n_new, m_new), h

    init = (jnp.zeros((Bn, H, Dh, Dh), F32), jnp.zeros((Bn, H, Dh), F32), jnp.zeros((Bn, H), F32))
    _, hs = lax.scan(step, init, xs)
    return jnp.moveaxis(hs, 0, 2).reshape(Bn, H, S, Dh)


def diff_attention(q, k, v, lam):
    Bn, S, H, _, dk = q.shape
    dv = v.shape[-1]
    nq = S // C_QBLK
    qb = jnp.moveaxis(q.reshape(Bn, nq, C_QBLK, H, 2, dk), 1, 0)
    scale = 1.0 / math.sqrt(dk)

    def block(qblk):
        s = jnp.einsum('bqhmd,bkhmd->bhmqk', qblk, k).astype(F32) * scale
        p = jax.nn.softmax(s, axis=-1)
        a = p[:, :, 0] - lam * p[:, :, 1]
        return jnp.einsum('bhqk,bkhe->bqhe', a.astype(v.dtype), v)

    o = lax.map(block, qb)
    return jnp.moveaxis(o, 0, 1).reshape(Bn, S, H, dv)


def encoder_layer(x, layer_idx, norm1_g, w_in, a_qn_g, a_kn_g, b_conv_w, b_conv_b, b_wq, b_wk, b_gate_bias,
                  b_hn_g, c_qn_g, c_kn_g, c_lambda, c_hn_g, w_branch, w_out, norm2_g, w_ffn_in, w_ffn_out):
    Bn, S, _ = x.shape
    dt = x.dtype
    cos, sin = rotary_tables(S, dt)
    xn = rms_norm(x, norm1_g)
    proj = xn @ w_in
    (a_q, a_k, a_v, b_c, b_v, b_o, b_g, c_q, c_k, c_v, gate_pre) = jnp.split(proj, IN_OFFSETS, axis=-1)

    shp = (Bn, S, A_GROUPS, A_HEADS, HEAD_DIM)
    qa = partial_rotary(rms_norm(a_q.reshape(shp), a_qn_g), cos, sin)
    ka = partial_rotary(rms_norm(a_k.reshape(shp), a_kn_g), cos, sin)
    va = a_v.reshape(shp)
    outs, lses = [], []
    for g, (window, dil) in enumerate(A_PATTERNS):
        o_g, l_g = dilated_window_attention(qa[:, :, g], ka[:, :, g], va[:, :, g], dil, window // (2 * dil))
        outs.append(o_g)
        lses.append(l_g)
    alpha = jax.nn.softmax(jnp.stack(lses, axis=0), axis=0)
    o_a = jnp.sum(alpha[..., None] * jnp.stack(outs, axis=0).astype(F32), axis=0).astype(dt)
    o_a = o_a.reshape(Bn, S, A_WIDTH)

    pad = B_CONV // 2
    cp = jnp.pad(b_c, ((0, 0), (pad, pad), (0, 0)))
    conv = b_conv_b
    for j in range(B_CONV):
        conv = conv + cp[:, j:j + S] * b_conv_w[j]
    u = jax.nn.silu(conv).reshape(Bn, S, B_HEADS, B_HEAD_DIM)
    qb = jnp.einsum('bshd,hde->bhse', u, b_wq)
    kb = jnp.einsum('bshd,hde->bhse', u, b_wk) * (1.0 / math.sqrt(B_HEAD_DIM))
    vb = b_v.reshape(Bn, S, B_HEADS, B_HEAD_DIM).transpose(0, 2, 1, 3)
    gts = (b_g.astype(F32).reshape(Bn, S, 4, B_HEADS) + b_gate_bias.astype(F32)).transpose(2, 0, 3, 1)
    h_fwd = mlstm_chunkwise(qb, kb, vb, gts[0], jax.nn.log_sigmoid(gts[1]))
    flip = lambda t: jnp.flip(t, axis=2)
    h_bwd = flip(mlstm_chunkwise(flip(qb), flip(kb), flip(vb), flip(gts[2]), jax.nn.log_sigmoid(flip(gts[3]))))
    hb = (h_fwd + h_bwd).transpose(0, 2, 1, 3)
    hb = rms_norm(hb, b_hn_g.reshape(B_HEADS, B_HEAD_DIM)).astype(dt)
    o_b = jax.nn.sigmoid(b_o) * hb.reshape(Bn, S, B_WIDTH)

    cshp = (Bn, S, C_HEADS, 2, C_QK_DIM)
    qc = partial_rotary(rms_norm(c_q.reshape(cshp), c_qn_g), cos, sin)
    kc = partial_rotary(rms_norm(c_k.reshape(cshp), c_kn_g), cos, sin)
    vc = c_v.reshape(Bn, S, C_HEADS, C_V_DIM)
    lam_init = 0.8 - 0.6 * math.exp(-0.3 * layer_idx)
    lf = c_lambda.astype(F32)
    lam = jnp.exp(jnp.sum(lf[0] * lf[1])) - jnp.exp(jnp.sum(lf[2] * lf[3])) + lam_init
    o_c = diff_attention(qc, kc, vc, lam)
    o_c = (rms_norm(o_c, c_hn_g) * (1.0 - lam_init)).reshape(Bn, S, C_WIDTH)

    gates = jax.nn.sigmoid(gate_pre).reshape(Bn, S, N_BRANCH, D_MODEL)
    y_a = o_a @ w_branch[:A_WIDTH]
    y_b = o_b @ w_branch[A_WIDTH:A_WIDTH + B_WIDTH]
    y_c = o_c @ w_branch[A_WIDTH + B_WIDTH:]
    mixed = gates[:, :, 0] * y_a + gates[:, :, 1] * y_b + gates[:, :, 2] * y_c
    x = x + mixed @ w_out

    hg, hu = jnp.split(rms_norm(x, norm2_g) @ w_ffn_in, 2, axis=-1)
    return x + (jax.nn.silu(hg) * hu) @ w_ffn_out


def run_trunk(x, weights):
    for layer in range(DEPTH):
        x = encoder_layer(x, layer, *[w[layer] for w in weights])
    return x


def setup_inputs(seed: int = 0) -> dict:
    key = jax.random.key(seed)
    ks = jax.random.split(key, 24)

    def nrm(k, shape, scale):
        return jax.random.normal(k, shape, F32) * scale

    def gain(k, shape):
        return 1.0 + 0.05 * jax.random.normal(k, shape, F32)

    forget_mask = jnp.array([0.0, 1.0, 0.0, 1.0], F32)[None, :, None]
    forget_init = jnp.linspace(3.0, 6.0, B_HEADS, dtype=F32)[None, None, :]
    return {
        "x_prompt": nrm(ks[0], (BATCH, SEQ, D_MODEL), 1.0),
        "x_sample": nrm(ks[1], (DEC_BATCH, DEC_SEQ, D_MODEL), 1.0),
        "norm1_g": gain(ks[2], (DEPTH, D_MODEL)),
        "w_in": nrm(ks[3], (DEPTH, D_MODEL, IN_COLS), D_MODEL ** -0.5),
        "a_qn_g": gain(ks[4], (DEPTH, HEAD_DIM)),
        "a_kn_g": gain(ks[5], (DEPTH, HEAD_DIM)),
        "b_conv_w": nrm(ks[6], (DEPTH, B_CONV, B_WIDTH), B_CONV ** -0.5),
        "b_conv_b": nrm(ks[7], (DEPTH, B_WIDTH), 0.02),
        "b_wq": nrm(ks[8], (DEPTH, B_HEADS, B_HEAD_DIM, B_HEAD_DIM), B_HEAD_DIM ** -0.5),
        "b_wk": nrm(ks[9], (DEPTH, B_HEADS, B_HEAD_DIM, B_HEAD_DIM), B_HEAD_DIM ** -0.5),
        "b_gate_bias": nrm(ks[10], (DEPTH, 4, B_HEADS), 0.1) + forget_mask * forget_init,
        "b_hn_g": gain(ks[11], (DEPTH, B_WIDTH)),
        "c_qn_g": gain(ks[12], (DEPTH, C_QK_DIM)),
        "c_kn_g": gain(ks[13], (DEPTH, C_QK_DIM)),
        "c_lambda": nrm(ks[14], (DEPTH, 4, C_QK_DIM), 0.1),
        "c_hn_g": gain(ks[15], (DEPTH, C_V_DIM)),
        "w_branch": nrm(ks[16], (DEPTH, BRANCH_WIDTH, D_MODEL), B_WIDTH ** -0.5),
        "w_out": nrm(ks[17], (DEPTH, D_MODEL, D_MODEL), D_MODEL ** -0.5),
        "norm2_g": gain(ks[18], (DEPTH, D_MODEL)),
        "w_ffn_in": nrm(ks[19], (DEPTH, D_MODEL, 2 * D_FF), D_MODEL ** -0.5),
        "w_ffn_out": nrm(ks[20], (DEPTH, D_FF, D_MODEL), D_FF ** -0.5),
    }


def reference(x_prompt, x_sample, norm1_g, w_in, a_qn_g, a_kn_g, b_conv_w, b_conv_b, b_wq, b_wk, b_gate_bias,
              b_hn_g, c_qn_g, c_kn_g, c_lambda, c_hn_g, w_branch, w_out, norm2_g, w_ffn_in, w_ffn_out):
    weights = (norm1_g, w_in, a_qn_g, a_kn_g, b_conv_w, b_conv_b, b_wq, b_wk, b_gate_bias, b_hn_g,
               c_qn_g, c_kn_g, c_lambda, c_hn_g, w_branch, w_out, norm2_g, w_ffn_in, w_ffn_out)
    y_prompt = run_trunk(x_prompt, weights)
    y_sample = run_trunk(x_sample, weights)
    return (y_prompt, y_sample)
```

```python
import functools
import math

import jax
import jax.numpy as jnp
from jax import lax
from jax.experimental import pallas as pl
from jax.experimental.pallas import tpu as pltpu

F32 = jnp.float32
BF16 = jnp.bfloat16

D_MODEL = 1024
DEPTH = 4
NORM_EPS = 1e-6
HEAD_DIM = 64
ROT_DIM = HEAD_DIM // 4
ROPE_THETA = 500000.0

A_PATTERNS = ((128, 1), (512, 4), (2048, 16))
A_GROUPS = len(A_PATTERNS)
A_HEADS = 6
A_WIDTH = A_HEADS * HEAD_DIM
A_QKV = A_GROUPS * A_WIDTH

B_HEADS = 4
B_HEAD_DIM = 128
B_WIDTH = B_HEADS * B_HEAD_DIM
B_GATES = 4 * B_HEADS

C_HEADS = 4
C_QK_DIM = HEAD_DIM
C_V_DIM = 2 * C_QK_DIM
C_QK_WIDTH = C_HEADS * 2 * C_QK_DIM
C_WIDTH = C_HEADS * C_V_DIM

N_BRANCH = 3
D_FF = -(-8 * D_MODEL // (3 * 256)) * 256

LANES = 128
VMEM_LIMIT = 56 * 1024 * 1024

COL_AQ, COL_AK, COL_AV = 0, 1152, 2304
COL_BG = 3456
COL_BC, COL_BV, COL_BO = 3584, 4096, 4608
COL_GATE = 5120
COL_CQ, COL_CK, COL_CV = 8192, 8704, 9216
PROJ_COLS = 9728
PROJ_TN = PROJ_COLS // 4

MLSTM_CHUNK = 128


def _cparams(sem):
    return pltpu.CompilerParams(dimension_semantics=sem, vmem_limit_bytes=VMEM_LIMIT)


def _norm_matmul_kernel(x_ref, g_ref, w_ref, o_ref):
    x = x_ref[...]
    ms = jnp.mean(x * x, axis=-1, keepdims=True)
    y = (x * lax.rsqrt(ms + NORM_EPS)) * g_ref[...]
    o_ref[...] = jnp.dot(y.astype(BF16), w_ref[...], preferred_element_type=F32)


def norm_matmul(x2d, g, w, tm=512, tn=PROJ_TN):
    T, K = x2d.shape
    N = w.shape[1]
    return pl.pallas_call(
        _norm_matmul_kernel,
        grid=(N // tn, T // tm),
        in_specs=[pl.BlockSpec((tm, K), lambda j, i: (i, 0)),
                  pl.BlockSpec((1, K), lambda j, i: (0, 0)),
                  pl.BlockSpec((K, tn), lambda j, i: (0, j))],
        out_specs=pl.BlockSpec((tm, tn), lambda j, i: (i, j)),
        out_shape=jax.ShapeDtypeStruct((T, N), F32),
        compiler_params=_cparams(("arbitrary", "arbitrary")),
        name="in_proj",
    )(x2d, g, w)


def _headnorm_rope(x, g, cos, s_lo, s_hi, bd, scale):
    x2 = x * x
    hi = x2.astype(BF16)
    lo = (x2 - hi.astype(F32)).astype(BF16)
    ss = jnp.dot(hi, bd, preferred_element_type=F32) + jnp.dot(lo, bd, preferred_element_type=F32)
    r = lax.rsqrt(ss * (1.0 / HEAD_DIM) + NORM_EPS)
    xg = x * g
    rot = (xg * cos + pltpu.roll(xg, LANES - ROT_DIM // 2, 1) * s_lo + pltpu.roll(xg, ROT_DIM // 2, 1) * s_hi)
    return rot * (r * scale)


def _prep_kernel(aq_ref, ak_ref, av_ref, cq_ref, ck_ref, cv_ref, cos_ref, slo_ref, shi_ref, bd_ref,
                 gaq_ref, gak_ref, gcq_ref, gck_ref,
                 oaq_ref, oak_ref, oav_ref, ocq_ref, ock_ref, ocv_ref):
    cos, s_lo, s_hi, bd = cos_ref[...], slo_ref[...], shi_ref[...], bd_ref[...]
    qscale = 1.0 / math.sqrt(HEAD_DIM)
    for src, dst, g_ref, scale in ((aq_ref, oaq_ref, gaq_ref, qscale), (ak_ref, oak_ref, gak_ref, 1.0),
                                   (cq_ref, ocq_ref, gcq_ref, qscale), (ck_ref, ock_ref, gck_ref, 1.0)):
        g = g_ref[...]
        for c in range(src.shape[1] // LANES):
            sl = slice(c * LANES, (c + 1) * LANES)
            dst[:, sl] = _headnorm_rope(src[:, sl], g, cos, s_lo, s_hi, bd, scale).astype(BF16)
    oav_ref[...] = av_ref[...].astype(BF16)
    ocv_ref[...] = cv_ref[...].astype(BF16)


def qk_prep(proj, tables, bd, gains, S, tm=256):
    T = proj.shape[0]
    nps = S // tm

    def col(width, off):
        return pl.BlockSpec((tm, width), lambda i: (i, off // width))

    tab = pl.BlockSpec((tm, LANES), lambda i: (i % nps, 0))
    full = lambda shp: pl.BlockSpec(shp, lambda i: (0, 0))
    outs = [(A_QKV, BF16)] * 3 + [(C_QK_WIDTH, BF16)] * 3
    return pl.pallas_call(
        _prep_kernel,
        grid=(T // tm,),
        in_specs=[col(A_QKV, COL_AQ), col(A_QKV, COL_AK), col(A_QKV, COL_AV),
                  col(C_QK_WIDTH, COL_CQ), col(C_QK_WIDTH, COL_CK), col(C_WIDTH, COL_CV),
                  tab, tab, tab, full((LANES, LANES))] + [full((1, LANES))] * 4,
        out_specs=[pl.BlockSpec((tm, w), lambda i: (i, 0)) for w, _ in outs],
        out_shape=[jax.ShapeDtypeStruct((T, w), dt) for w, dt in outs],
        compiler_params=_cparams(("arbitrary",)),
        name="qk_prep",
    )(proj, proj, proj, proj, proj, proj, *tables, bd, *gains)


def _mixa_kernel(q_ref, kp_ref, kc_ref, kn_ref, vp_ref, vc_ref, vn_ref, o_ref, l_ref, *, tq, L, half):
    i = pl.program_id(2)
    row = lax.broadcasted_iota(jnp.int32, (tq, 3 * tq), 0)
    col = lax.broadcasted_iota(jnp.int32, (tq, 3 * tq), 1)
    kpos = (i - 1) * tq + col
    delta = col - row - tq
    inside = jnp.where(kpos >= 0, jnp.where(kpos < L, 1, 0), 0)
    mask = jnp.where(jnp.abs(delta) <= half, inside, 0) > 0
    k_all = jnp.concatenate([kp_ref[...], kc_ref[...], kn_ref[...]], axis=0)
    v_all = jnp.concatenate([vp_ref[...], vc_ref[...], vn_ref[...]], axis=0)
    q = q_ref[...]
    for h in range(A_HEADS):
        sl = slice(h * HEAD_DIM, (h + 1) * HEAD_DIM)
        s = lax.dot_general(q[:, sl], k_all[:, sl], (((1,), (1,)), ((), ())), preferred_element_type=F32)
        s = jnp.where(mask, s, -1e30)
        mx = jnp.max(s, axis=-1, keepdims=True)
        p = jnp.exp(s - mx)
        den = jnp.sum(p, axis=-1, keepdims=True)
        o = jnp.dot((p / den).astype(BF16), v_all[:, sl], preferred_element_type=F32)
        o_ref[:, sl] = o
        l_ref[:, sl] = jnp.broadcast_to(mx + jnp.log(den), (tq, HEAD_DIM))


def mixer_a_group(qa, ka, va, g, Bn, S, tq=128):
    window, dil = A_PATTERNS[g]
    half = window // (2 * dil)
    L = S // dil
    tq = min(tq, L)
    nq = L // tq
    q3, k3, v3 = (t.reshape(Bn, L, dil * A_QKV) for t in (qa, ka, va))
    ngrp = A_QKV // A_WIDTH

    def spec(shift):
        return pl.BlockSpec((None, tq, A_WIDTH),
                            lambda b, r, i: (b, jnp.clip(i + shift, 0, nq - 1), r * ngrp + g))

    out_spec = pl.BlockSpec((None, tq, A_WIDTH), lambda b, r, i: (b, i, r))
    o, lse = pl.pallas_call(
        functools.partial(_mixa_kernel, tq=tq, L=L, half=half),
        grid=(Bn, dil, nq),
        in_specs=[spec(0), spec(-1), spec(0), spec(1), spec(-1), spec(0), spec(1)],
        out_specs=[out_spec, out_spec],
        out_shape=[jax.ShapeDtypeStruct((Bn, L, dil * A_WIDTH), F32)] * 2,
        compiler_params=_cparams(("arbitrary", "arbitrary", "arbitrary")),
        name=f"mixer_a_g{g}",
    )(q3, k3, k3, k3, v3, v3, v3)
    return o.reshape(Bn * S, A_WIDTH), lse.reshape(Bn * S, A_WIDTH)


def _mixc_kernel(q_ref, k_ref, v_ref, lam_ref, laminit_ref, g_ref, o_ref, *, tq, tk, S):
    q = q_ref[...]
    nk = S // tk

    def one_map(qm, km, vb, m, l, a):
        s = lax.dot_general(qm, km, (((1,), (1,)), ((), ())), preferred_element_type=F32)
        mn = jnp.maximum(m, jnp.max(s, axis=-1, keepdims=True))
        alpha = jnp.exp(m - mn)
        p = jnp.exp(s - mn)
        l = alpha * l + jnp.sum(p, axis=-1, keepdims=True)
        a = alpha * a + jnp.dot(p.astype(BF16), vb, preferred_element_type=F32)
        return mn, l, a

    def body(j, carry):
        m1, l1, a1, m2, l2, a2 = carry
        off = pl.multiple_of(j * tk, tk)
        kb = k_ref[pl.ds(off, tk), :]
        vb = v_ref[pl.ds(off, tk), :]
        m1, l1, a1 = one_map(q[:, :C_QK_DIM], kb[:, :C_QK_DIM], vb, m1, l1, a1)
        m2, l2, a2 = one_map(q[:, C_QK_DIM:], kb[:, C_QK_DIM:], vb, m2, l2, a2)
        return m1, l1, a1, m2, l2, a2

    m0 = jnp.full((tq, 1), -1e30, F32)
    l0 = jnp.zeros((tq, 1), F32)
    a0 = jnp.zeros((tq, C_V_DIM), F32)
    m1, l1, a1, m2, l2, a2 = lax.fori_loop(0, nk, body, (m0, l0, a0, m0, l0, a0))
    lm = lam_ref[...]
    lam = (jnp.exp(jnp.sum(lm[0:1] * lm[1:2], axis=-1, keepdims=True))
           - jnp.exp(jnp.sum(lm[2:3] * lm[3:4], axis=-1, keepdims=True)) + laminit_ref[...])
    o = a1 / l1 - lam * (a2 / l2)
    ms = jnp.mean(o * o, axis=-1, keepdims=True)
    o_ref[...] = ((o * lax.rsqrt(ms + NORM_EPS)) * g_ref[...]).astype(BF16)


def mixer_c(qc, kc, vc, c_lambda, lam_init, g_eff, Bn, S, tq=256, tk=512):
    q3, k3, v3 = (t.reshape(Bn, S, C_QK_WIDTH) for t in (qc, kc, vc))
    o = pl.pallas_call(
        functools.partial(_mixc_kernel, tq=tq, tk=tk, S=S),
        grid=(Bn, C_HEADS, S // tq),
        in_specs=[pl.BlockSpec((None, tq, LANES), lambda b, h, i: (b, i, h)),
                  pl.BlockSpec((None, S, LANES), lambda b, h, i: (b, 0, h)),
                  pl.BlockSpec((None, S, LANES), lambda b, h, i: (b, 0, h)),
                  pl.BlockSpec((4, C_QK_DIM), lambda b, h, i: (0, 0)),
                  pl.BlockSpec((1, 1), lambda b, h, i: (0, 0)),
                  pl.BlockSpec((1, LANES), lambda b, h, i: (0, 0))],
        out_specs=pl.BlockSpec((None, tq, LANES), lambda b, h, i: (b, i, h)),
        out_shape=jax.ShapeDtypeStruct((Bn, S, C_WIDTH), BF16),
        compiler_params=_cparams(("arbitrary", "arbitrary", "arbitrary")),
        name="mixer_c",
    )(q3, k3, v3, c_lambda, lam_init, g_eff)
    return o.reshape(Bn * S, C_WIDTH)


def _bprep_kernel(c_ref, cp_ref, cn_ref, cw_ref, cb_ref, wq_ref, wk_ref, q_ref, k_ref, *, ts, nt):
    i = pl.program_id(1)
    c = c_ref[...]
    row = lax.broadcasted_iota(jnp.int32, (ts, 1), 0)
    prev_row = cp_ref[7:8, :] * jnp.where(i > 0, 1.0, 0.0)
    next_row = cn_ref[0:1, :] * jnp.where(i < nt - 1, 1.0, 0.0)
    c_prev = jnp.where(row == 0, prev_row, pltpu.roll(c, 1, 0))
    c_next = jnp.where(row == ts - 1, next_row, pltpu.roll(c, ts - 1, 0))
    cw = cw_ref[...]
    conv = cb_ref[...] + c_prev * cw[0:1] + c * cw[1:2] + c_next * cw[2:3]
    u = (conv * jax.nn.sigmoid(conv)).astype(BF16)
    kscale = 1.0 / math.sqrt(B_HEAD_DIM)
    for h in range(B_HEADS):
        sl = slice(h * B_HEAD_DIM, (h + 1) * B_HEAD_DIM)
        q_ref[:, sl] = jnp.dot(u[:, sl], wq_ref[h], preferred_element_type=F32).astype(BF16)
        k_ref[:, sl] = jnp.dot(u[:, sl], wk_ref[h], preferred_element_type=F32) * kscale


def mixer_b_prep(proj3, conv_w, conv_b, wq, wk, Bn, S, ts=512):
    nt = S // ts
    r8 = ts // 8
    nb8 = S // 8
    return pl.pallas_call(
        functools.partial(_bprep_kernel, ts=ts, nt=nt),
        grid=(Bn, nt),
        in_specs=[pl.BlockSpec((None, ts, B_WIDTH), lambda b, i: (b, i, COL_BC // B_WIDTH)),
                  pl.BlockSpec((None, 8, B_WIDTH), lambda b, i: (b, jnp.maximum(i * r8 - 1, 0), COL_BC // B_WIDTH)),
                  pl.BlockSpec((None, 8, B_WIDTH),
                               lambda b, i: (b, jnp.minimum((i + 1) * r8, nb8 - 1), COL_BC // B_WIDTH)),
                  pl.BlockSpec((3, B_WIDTH), lambda b, i: (0, 0)),
                  pl.BlockSpec((1, B_WIDTH), lambda b, i: (0, 0)),
                  pl.BlockSpec((B_HEADS, B_HEAD_DIM, B_HEAD_DIM), lambda b, i: (0, 0, 0)),
                  pl.BlockSpec((B_HEADS, B_HEAD_DIM, B_HEAD_DIM), lambda b, i: (0, 0, 0))],
        out_specs=[pl.BlockSpec((None, ts, B_WIDTH), lambda b, i: (b, i, 0))] * 2,
        out_shape=[jax.ShapeDtypeStruct((Bn, S, B_WIDTH), BF16), jax.ShapeDtypeStruct((Bn, S, B_WIDTH), F32)],
        compiler_params=_cparams(("arbitrary", "arbitrary")),
        name="mixer_b_prep",
    )(proj3, proj3, proj3, conv_w, conv_b, wq, wk)


def _split3(x):
    h1 = x.astype(BF16)
    r1 = x - h1.astype(F32)
    h2 = r1.astype(BF16)
    h3 = (r1 - h2.astype(F32)).astype(BF16)
    return h1, h2, h3


def _mlstm_chunk(q, k, v, li_col, li_row, b_col, b_row, b_last, causal_mask, C, n, m):
    logD = jnp.where(causal_mask, b_col - b_row + li_row, -jnp.inf)
    inter = b_col + m
    m_t = jnp.maximum(inter, jnp.max(logD, axis=-1, keepdims=True))
    qk = lax.dot_general(q, k.astype(BF16), (((1,), (1,)), ((), ())), preferred_element_type=F32)
    sc = qk * jnp.exp(logD - m_t)
    w_inter = jnp.exp(inter - m_t)
    num = (w_inter * jnp.dot(q, C.astype(BF16), preferred_element_type=F32)
           + jnp.dot(sc.astype(BF16), v.astype(BF16), preferred_element_type=F32))
    qn = jnp.sum(q.astype(F32) * n, axis=-1, keepdims=True)
    den = w_inter * qn + jnp.sum(sc, axis=-1, keepdims=True)
    h = num / jnp.maximum(jnp.abs(den), jnp.exp(-m_t))
    m_new = jnp.maximum(b_last + m, jnp.max(b_last - b_row + li_row, axis=-1, keepdims=True))
    decay = jnp.exp(b_last + m - m_new)
    wk = jnp.exp(b_last - b_col + li_col - m_new) * k
    C_new = decay * C + lax.dot_general(wk.astype(BF16), v.astype(BF16), (((0,), (0,)), ((), ())),
                                        preferred_element_type=F32)
    n_new = decay * n + jnp.sum(wk, axis=0, keepdims=True)
    return h, C_new, n_new, m_new


def _mlstm_kernel(qf_ref, kf_ref, vf_ref, gf_ref, qb_ref, kb_ref, vb_ref, gb_ref, bias_ref,
                  hf_ref, hb_ref, c_scr, n_scr, m_scr, *, ts):
    L = MLSTM_CHUNK
    nch = ts // L

    @pl.when(pl.program_id(1) == 0)
    def _():
        c_scr[...] = jnp.zeros_like(c_scr)
        n_scr[...] = jnp.zeros_like(n_scr)
        m_scr[...] = jnp.zeros_like(m_scr)

    row = lax.broadcasted_iota(jnp.int32, (L, L), 0)
    col = lax.broadcasted_iota(jnp.int32, (L, L), 1)
    lower = (col <= row)
    upper = (col >= row)
    tri_f = jnp.where(lower, 1.0, 0.0).astype(BF16)
    tri_b = jnp.where(upper, 1.0, 0.0).astype(BF16)
    bias = bias_ref[...]

    def direction(d, q_ref, k_ref, v_ref, g_ref, h_ref, tri, mask, chunk_order, last_row):
        for cidx in chunk_order:
            rows = slice(cidx * L, (cidx + 1) * L)
            g = g_ref[rows, :] + bias
            lf = -(jnp.maximum(-g, 0.0) + jnp.log1p(jnp.exp(-jnp.abs(g))))
            s1, s2, s3 = _split3(lf)
            cum = (jnp.dot(tri, s1, preferred_element_type=F32) + jnp.dot(tri, s2, preferred_element_type=F32)
                   + jnp.dot(tri, s3, preferred_element_type=F32))
            g_t = g.T
            cum_t = cum.T
            for h in range(B_HEADS):
                ci = (2 * d) * B_HEADS + h
                cf = (2 * d + 1) * B_HEADS + h
                sl = slice(h * B_HEAD_DIM, (h + 1) * B_HEAD_DIM)
                b_col = cum[:, cf:cf + 1]
                hh, c_new, n_new, m_new = _mlstm_chunk(
                    q_ref[rows, sl], k_ref[rows, sl], v_ref[rows, sl],
                    g[:, ci:ci + 1], g_t[ci:ci + 1, :], b_col, cum_t[cf:cf + 1, :],
                    b_col[last_row:last_row + 1, :], mask,
                    c_scr[d, h], n_scr[d, h], m_scr[d, h])
                h_ref[rows, sl] = hh
                c_scr[d, h] = c_new
                n_scr[d, h] = n_new
                m_scr[d, h] = m_new

    direction(0, qf_ref, kf_ref, vf_ref, gf_ref, hf_ref, tri_f, lower, range(nch), L - 1)
    direction(1, qb_ref, kb_ref, vb_ref, gb_ref, hb_ref, tri_b, upper, range(nch - 1, -1, -1), 0)


def mixer_b(qb, kb, proj3, gate_bias, Bn, S, ts=256):
    nt = S // ts
    fwd = lambda blk: (lambda b, i: (b, i, blk))
    bwd = lambda blk: (lambda b, i: (b, nt - 1 - i, blk))

    def specs(idx):
        return [pl.BlockSpec((None, ts, B_WIDTH), idx(0)),
                pl.BlockSpec((None, ts, B_WIDTH), idx(0)),
                pl.BlockSpec((None, ts, B_WIDTH), idx(COL_BV // B_WIDTH)),
                pl.BlockSpec((None, ts, LANES), idx(COL_BG // LANES))]

    return pl.pallas_call(
        functools.partial(_mlstm_kernel, ts=ts),
        grid=(Bn, nt),
        in_specs=specs(fwd) + specs(bwd) + [pl.BlockSpec((1, LANES), lambda b, i: (0, 0))],
        out_specs=[pl.BlockSpec((None, ts, B_WIDTH), fwd(0)), pl.BlockSpec((None, ts, B_WIDTH), bwd(0))],
        out_shape=[jax.ShapeDtypeStruct((Bn, S, B_WIDTH), F32)] * 2,
        scratch_shapes=[pltpu.VMEM((2, B_HEADS, B_HEAD_DIM, B_HEAD_DIM), F32),
                        pltpu.VMEM((2, B_HEADS, 1, B_HEAD_DIM), F32),
                        pltpu.VMEM((2, B_HEADS, 1, 1), F32)],
        compiler_params=_cparams(("arbitrary", "arbitrary")),
        name="mixer_b",
    )(qb, kb, proj3, proj3, qb, kb, proj3, proj3, gate_bias)


def _merge_kernel(x_ref, g0_ref, g1_ref, g2_ref, oa0_ref, oa1_ref, oa2_ref, la0_ref, la1_ref, la2_ref,
                  hf_ref, hb_ref, bo_ref, oc_ref, hng_ref, wa_ref, wb_ref, wc_ref, wo_ref, y_ref):
    l0, l1, l2 = la0_ref[...], la1_ref[...], la2_ref[...]
    mx = jnp.maximum(jnp.maximum(l0, l1), l2)
    e0, e1, e2 = jnp.exp(l0 - mx), jnp.exp(l1 - mx), jnp.exp(l2 - mx)
    o_a = (e0 * oa0_ref[...] + e1 * oa1_ref[...] + e2 * oa2_ref[...]) / (e0 + e1 + e2)
    y_a = jnp.dot(o_a.astype(BF16), wa_ref[...], preferred_element_type=F32)
    hsum = hf_ref[...] + hb_ref[...]
    hng = hng_ref[...]
    bo = bo_ref[...]
    parts = []
    for h in range(B_HEADS):
        sl = slice(h * B_HEAD_DIM, (h + 1) * B_HEAD_DIM)
        hh = hsum[:, sl]
        ms = jnp.mean(hh * hh, axis=-1, keepdims=True)
        parts.append(jax.nn.sigmoid(bo[:, sl]) * ((hh * lax.rsqrt(ms + NORM_EPS)) * hng[:, sl]))
    o_b = jnp.concatenate(parts, axis=-1)
    y_b = jnp.dot(o_b.astype(BF16), wb_ref[...], preferred_element_type=F32)
    y_c = jnp.dot(oc_ref[...], wc_ref[...], preferred_element_type=F32)
    mixed = (jax.nn.sigmoid(g0_ref[...]) * y_a + jax.nn.sigmoid(g1_ref[...]) * y_b
             + jax.nn.sigmoid(g2_ref[...]) * y_c)
    y_ref[...] = x_ref[...] + jnp.dot(mixed.astype(BF16), wo_ref[...], preferred_element_type=F32)


def merge(x2d, proj, oa, la, hf, hb, oc, hng, wa, wb, wc, wo, tm=256):
    T = x2d.shape[0]
    row = lambda w, blk=0: pl.BlockSpec((tm, w), lambda i: (i, blk))
    full = lambda a: pl.BlockSpec(a.shape, lambda i: (0, 0))
    gate = lambda j: row(D_MODEL, COL_GATE // D_MODEL + j)
    return pl.pallas_call(
        _merge_kernel,
        grid=(T // tm,),
        in_specs=[row(D_MODEL), gate(0), gate(1), gate(2)] + [row(A_WIDTH)] * 6
                 + [row(B_WIDTH), row(B_WIDTH), row(B_WIDTH, COL_BO // B_WIDTH), row(C_WIDTH),
                    full(hng), full(wa), full(wb), full(wc), full(wo)],
        out_specs=row(D_MODEL),
        out_shape=jax.ShapeDtypeStruct((T, D_MODEL), F32),
        compiler_params=_cparams(("arbitrary",)),
        name="merge",
    )(x2d, proj, proj, proj, *oa, *la, hf, hb, proj, oc, hng, wa, wb, wc, wo)


FFN_CHUNK = D_FF // 2


def _ffn_kernel(x_ref, g_ref, wg_ref, wu_ref, wd_ref, y_ref):
    x = x_ref[...]
    ms = jnp.mean(x * x, axis=-1, keepdims=True)
    xn = ((x * lax.rsqrt(ms + NORM_EPS)) * g_ref[...]).astype(BF16)
    acc = x
    for c in range(D_FF // FFN_CHUNK):
        sl = slice(c * FFN_CHUNK, (c + 1) * FFN_CHUNK)
        hg = jnp.dot(xn, wg_ref[:, sl], preferred_element_type=F32)
        hu = jnp.dot(xn, wu_ref[:, sl], preferred_element_type=F32)
        a = (hg * jax.nn.sigmoid(hg)) * hu
        acc = acc + jnp.dot(a.astype(BF16), wd_ref[sl, :], preferred_element_type=F32)
    y_ref[...] = acc


def ffn(x2d, g, wg, wu, wd, tm=256):
    T = x2d.shape[0]
    full = lambda a: pl.BlockSpec(a.shape, lambda i: (0, 0))
    return pl.pallas_call(
        _ffn_kernel,
        grid=(T // tm,),
        in_specs=[pl.BlockSpec((tm, D_MODEL), lambda i: (i, 0)), full(g), full(wg), full(wu), full(wd)],
        out_specs=pl.BlockSpec((tm, D_MODEL), lambda i: (i, 0)),
        out_shape=jax.ShapeDtypeStruct((T, D_MODEL), F32),
        compiler_params=_cparams(("arbitrary",)),
        name="ffn",
    )(x2d, g, wg, wu, wd)


def _rotary_tables(S):
    inv = ROPE_THETA ** (-jnp.arange(0, ROT_DIM, 2, dtype=F32) / ROT_DIM)
    ang = jnp.arange(S, dtype=F32)[:, None] * inv[None, :]
    cos, sin = jnp.cos(ang), jnp.sin(ang)
    half = ROT_DIM // 2
    rest = HEAD_DIM - ROT_DIM
    one, zero, z8 = jnp.ones((S, rest), F32), jnp.zeros((S, rest), F32), jnp.zeros((S, half), F32)
    c64 = jnp.concatenate([cos, cos, one], axis=1)
    lo64 = jnp.concatenate([-sin, z8, zero], axis=1)
    hi64 = jnp.concatenate([z8, sin, zero], axis=1)
    return tuple(jnp.tile(t, (1, LANES // HEAD_DIM)) for t in (c64, lo64, hi64))


def _pack_layer(norm1_g, w_in, a_qn_g, a_kn_g, b_conv_w, b_conv_b, b_wq, b_wk, b_gate_bias, b_hn_g,
                c_qn_g, c_kn_g, c_lambda, c_hn_g, w_branch, w_out, norm2_g, w_ffn_in, w_ffn_out, layer_idx):
    o_bc = 3 * A_QKV
    o_bg = o_bc + 3 * B_WIDTH
    o_cq = o_bg + B_GATES
    o_gate = o_cq + 3 * C_QK_WIDTH
    w_re = jnp.concatenate([w_in[:, :o_bc], w_in[:, o_bg:o_cq], jnp.zeros((D_MODEL, LANES - B_GATES), F32),
                            w_in[:, o_bc:o_bg], w_in[:, o_gate:], w_in[:, o_cq:o_gate]], axis=1).astype(BF16)
    lam_init = 0.8 - 0.6 * math.exp(-0.3 * layer_idx)
    tile2 = lambda g: jnp.tile(g.reshape(1, HEAD_DIM), (1, LANES // HEAD_DIM))
    return dict(
        norm1_g=norm1_g.reshape(1, D_MODEL), w_in=w_re,
        gains=(tile2(a_qn_g), tile2(a_kn_g), tile2(c_qn_g), tile2(c_kn_g)),
        conv_w=b_conv_w, conv_b=b_conv_b.reshape(1, B_WIDTH), wq=b_wq.astype(BF16), wk=b_wk.astype(BF16),
        gate_bias=jnp.concatenate([b_gate_bias.reshape(1, B_GATES), jnp.zeros((1, LANES - B_GATES), F32)], axis=1),
        hn_g=b_hn_g.reshape(1, B_WIDTH),
        c_lambda=c_lambda, lam_init=jnp.full((1, 1), lam_init, F32),
        c_hn_g=(c_hn_g * (1.0 - lam_init)).reshape(1, C_V_DIM),
        wa=w_branch[:A_WIDTH].astype(BF16), wb=w_branch[A_WIDTH:A_WIDTH + B_WIDTH].astype(BF16),
        wc=w_branch[A_WIDTH + B_WIDTH:].astype(BF16), wo=w_out.astype(BF16),
        norm2_g=norm2_g.reshape(1, D_MODEL),
        wg=w_ffn_in[:, :D_FF].astype(BF16), wu=w_ffn_in[:, D_FF:].astype(BF16), wd=w_ffn_out.astype(BF16))


def _layer(x2d, p, tables, bd, Bn, S):
    proj = norm_matmul(x2d, p["norm1_g"], p["w_in"])
    qa, ka, va, qc, kc, vc = qk_prep(proj, tables, bd, p["gains"], S)
    a_out = [mixer_a_group(qa, ka, va, g, Bn, S) for g in range(A_GROUPS)]
    proj3 = proj.reshape(Bn, S, PROJ_COLS)
    qb, kb = mixer_b_prep(proj3, p["conv_w"], p["conv_b"], p["wq"], p["wk"], Bn, S)
    hf, hb = mixer_b(qb, kb, proj3, p["gate_bias"], Bn, S)
    oc = mixer_c(qc, kc, vc, p["c_lambda"], p["lam_init"], p["c_hn_g"], Bn, S)
    x2d = merge(x2d, proj, [o for o, _ in a_out], [l for _, l in a_out],
                hf.reshape(Bn * S, B_WIDTH), hb.reshape(Bn * S, B_WIDTH), oc,
                p["hn_g"], p["wa"], p["wb"], p["wc"], p["wo"])
    return ffn(x2d, p["norm2_g"], p["wg"], p["wu"], p["wd"])


def _trunk(x, layers, bd):
    Bn, S, _ = x.shape
    tables = _rotary_tables(S)
    x2d = x.reshape(Bn * S, D_MODEL)
    for p in layers:
        x2d = _layer(x2d, p, tables, bd, Bn, S)
    return x2d.reshape(Bn, S, D_MODEL)


def kernel(x_prompt, x_sample, norm1_g, w_in, a_qn_g, a_kn_g, b_conv_w, b_conv_b, b_wq, b_wk, b_gate_bias, b_hn_g,
           c_qn_g, c_kn_g, c_lambda, c_hn_g, w_branch, w_out, norm2_g, w_ffn_in, w_ffn_out):
    weights = (norm1_g, w_in, a_qn_g, a_kn_g, b_conv_w, b_conv_b, b_wq, b_wk, b_gate_bias, b_hn_g,
               c_qn_g, c_kn_g, c_lambda, c_hn_g, w_branch, w_out, norm2_g, w_ffn_in, w_ffn_out)
    depth = norm1_g.shape[0]
    layers = [_pack_layer(*[w[l] for w in weights], layer_idx=l) for l in range(depth)]
    lane = jnp.arange(LANES) // HEAD_DIM
    bd = (lane[:, None] == lane[None, :]).astype(BF16)
    return (_trunk(x_prompt, layers, bd), _trunk(x_sample, layers, bd))
```

```python
import functools
import math

import jax
import jax.numpy as jnp
from jax import lax
from jax.experimental import pallas as pl
from jax.experimental.pallas import tpu as pltpu

F32 = jnp.float32
BF16 = jnp.bfloat16

D_MODEL = 1024
DEPTH = 4
NORM_EPS = 1e-6
HEAD_DIM = 64
ROT_DIM = HEAD_DIM // 4
ROPE_THETA = 500000.0

A_PATTERNS = ((128, 1), (512, 4), (2048, 16))
A_GROUPS = len(A_PATTERNS)
A_HEADS = 6
A_WIDTH = A_HEADS * HEAD_DIM
A_QKV = A_GROUPS * A_WIDTH

B_HEADS = 4
B_HEAD_DIM = 128
B_WIDTH = B_HEADS * B_HEAD_DIM
B_GATES = 4 * B_HEADS

C_HEADS = 4
C_QK_DIM = HEAD_DIM
C_V_DIM = 2 * C_QK_DIM
C_QK_WIDTH = C_HEADS * 2 * C_QK_DIM
C_WIDTH = C_HEADS * C_V_DIM

N_BRANCH = 3
D_FF = -(-8 * D_MODEL // (3 * 256)) * 256

LANES = 128
VMEM_LIMIT = 56 * 1024 * 1024

COL_AQ, COL_AK, COL_AV = 0, 1152, 2304
COL_BG = 3456
COL_BC, COL_BV, COL_BO = 3584, 4096, 4608
COL_GATE = 5120
COL_CQ, COL_CK, COL_CV = 8192, 8704, 9216
PROJ_COLS = 9728
PROJ_TN = PROJ_COLS // 4

MLSTM_CHUNK = 128


def _cparams(sem):
    return pltpu.CompilerParams(dimension_semantics=sem, vmem_limit_bytes=VMEM_LIMIT)


def _norm_matmul_kernel(x_ref, g_ref, w_ref, o_ref):
    x = x_ref[...]
    ms = jnp.mean(x * x, axis=-1, keepdims=True)
    y = (x * lax.rsqrt(ms + NORM_EPS)) * g_ref[...]
    o_ref[...] = jnp.dot(y.astype(BF16), w_ref[...], preferred_element_type=F32)


def norm_matmul(x2d, g, w, tm=512, tn=PROJ_TN):
    T, K = x2d.shape
    N = w.shape[1]
    return pl.pallas_call(
        _norm_matmul_kernel,
        grid=(N // tn, T // tm),
        in_specs=[pl.BlockSpec((tm, K), lambda j, i: (i, 0)),
                  pl.BlockSpec((1, K), lambda j, i: (0, 0)),
                  pl.BlockSpec((K, tn), lambda j, i: (0, j))],
        out_specs=pl.BlockSpec((tm, tn), lambda j, i: (i, j)),
        out_shape=jax.ShapeDtypeStruct((T, N), F32),
        compiler_params=_cparams(("arbitrary", "arbitrary")),
        name="in_proj",
    )(x2d, g, w)


def _headnorm_rope(x, g, cos, s_lo, s_hi, bd, scale):
    x2 = x * x
    hi = x2.astype(BF16)
    lo = (x2 - hi.astype(F32)).astype(BF16)
    ss = jnp.dot(hi, bd, preferred_element_type=F32) + jnp.dot(lo, bd, preferred_element_type=F32)
    r = lax.rsqrt(ss * (1.0 / HEAD_DIM) + NORM_EPS)
    xg = x * g
    rot = (xg * cos + pltpu.roll(xg, LANES - ROT_DIM // 2, 1) * s_lo + pltpu.roll(xg, ROT_DIM // 2, 1) * s_hi)
    return rot * (r * scale)


A_PAIRS = A_HEADS // 2
QSCALE = math.log2(math.e) / math.sqrt(HEAD_DIM)


def _prep_kernel(aq_ref, ak_ref, av_ref, cq_ref, ck_ref, cv_ref, cos_ref, slo_ref, shi_ref, bd_ref,
                 gaq_ref, gak_ref, gcq_ref, gck_ref,
                 q0_ref, k0_ref, v0_ref, q1_ref, k1_ref, v1_ref, q2_ref, k2_ref, v2_ref,
                 ocq_ref, ock_ref, ocv_ref, relay_scr, *, tm):
    cos, s_lo, s_hi, bd = cos_ref[...], slo_ref[...], shi_ref[...], bd_ref[...]
    lane = lax.broadcasted_iota(jnp.int32, (tm, LANES), 1)

    def put(dst, idx, dil, y):
        if dil == 1:
            dst[idx, 0] = y.astype(BF16)
            return
        relay_scr[...] = y
        for r in range(dil):
            dst[idx, r] = relay_scr[pl.ds(r, tm // dil, stride=dil), :].astype(BF16)

    for src, dsts, g_ref, scale in ((aq_ref, (q0_ref, q1_ref, q2_ref), gaq_ref, QSCALE),
                                    (ak_ref, (k0_ref, k1_ref, k2_ref), gak_ref, 1.0)):
        g = g_ref[...]
        for c in range(A_QKV // LANES):
            grp, pair = divmod(c, A_PAIRS)
            y = _headnorm_rope(src[:, c * LANES:(c + 1) * LANES], g, cos, s_lo, s_hi, bd, scale)
            put(dsts[grp], pair, A_PATTERNS[grp][1], y)
    for c in range(A_QKV // LANES):
        grp, pair = divmod(c, A_PAIRS)
        y = av_ref[:, c * LANES:(c + 1) * LANES]
        dst = (v0_ref, v1_ref, v2_ref)[grp]
        put(dst, 2 * pair, A_PATTERNS[grp][1], jnp.where(lane < HEAD_DIM, y, 1.0))
        put(dst, 2 * pair + 1, A_PATTERNS[grp][1], jnp.where(lane < HEAD_DIM, pltpu.roll(y, HEAD_DIM, 1), 1.0))
    for src, dst, g_ref, scale in ((cq_ref, ocq_ref, gcq_ref, QSCALE), (ck_ref, ock_ref, gck_ref, 1.0)):
        g = g_ref[...]
        for c in range(C_QK_WIDTH // LANES):
            sl = slice(c * LANES, (c + 1) * LANES)
            dst[:, sl] = _headnorm_rope(src[:, sl], g, cos, s_lo, s_hi, bd, scale).astype(BF16)
    ocv_ref[...] = cv_ref[...].astype(BF16)


def qk_prep(proj, tables, bd, gains, Bn, S, tm=256):
    T = proj.shape[0]
    nps = S // tm

    def col(width, off):
        return pl.BlockSpec((tm, width), lambda i: (i, off // width))

    tab = pl.BlockSpec((tm, LANES), lambda i: (i % nps, 0))
    full = lambda shp: pl.BlockSpec(shp, lambda i: (0, 0))
    out_specs, out_shape = [], []
    for _, dil in A_PATTERNS:
        for n in (A_PAIRS, A_PAIRS, A_HEADS):
            out_specs.append(pl.BlockSpec((None, n, dil, tm // dil, LANES), lambda i: (i // nps, 0, 0, i % nps, 0)))
            out_shape.append(jax.ShapeDtypeStruct((Bn, n, dil, S // dil, LANES), BF16))
    for _ in range(3):
        out_specs.append(pl.BlockSpec((tm, C_QK_WIDTH), lambda i: (i, 0)))
        out_shape.append(jax.ShapeDtypeStruct((T, C_QK_WIDTH), BF16))
    return pl.pallas_call(
        functools.partial(_prep_kernel, tm=tm),
        grid=(T // tm,),
        in_specs=[col(A_QKV, COL_AQ), col(A_QKV, COL_AK), col(A_QKV, COL_AV),
                  col(C_QK_WIDTH, COL_CQ), col(C_QK_WIDTH, COL_CK), col(C_WIDTH, COL_CV),
                  tab, tab, tab, full((LANES, LANES))] + [full((1, LANES))] * 4,
        out_specs=out_specs,
        out_shape=out_shape,
        scratch_shapes=[pltpu.VMEM((tm, LANES), F32)],
        compiler_params=_cparams(("arbitrary",)),
        name="qk_prep",
    )(proj, proj, proj, proj, proj, proj, *tables, bd, *gains)


A_HALF = 64
assert all(w // (2 * d) == A_HALF for w, d in A_PATTERNS)
A_TQ = 256
A_FIN = 512


def _mixa_kernel(shift_ref, q0_ref, k0_ref, v0_ref, q1_ref, k1_ref, v1_ref, q2_ref, k2_ref, v2_ref, o_ref,
                 acc_scr, m_scr, *, S):
    qkv = ((q0_ref, k0_ref, v0_ref), (q1_ref, k1_ref, v1_ref), (q2_ref, k2_ref, v2_ref))
    bound = shift_ref[0]
    dn_nt = (((1,), (1,)), ((), ()))

    def sweep(mode):
        for g, (_, dil) in enumerate(A_PATTERNS):
            q_ref, k_ref, v_ref = qkv[g]
            L = S // dil
            tq = min(A_TQ, L)
            W = min(tq + 2 * A_HALF, L)
            nq = L // tq
            col_minus_row = (lax.broadcasted_iota(jnp.int32, (tq, W), 1)
                             - lax.broadcasted_iota(jnp.int32, (tq, W), 0))

            def tile(t, carry, g=g, dil=dil, L=L, tq=tq, W=W, nq=nq, q_ref=q_ref, k_ref=k_ref, v_ref=v_ref,
                     col_minus_row=col_minus_row):
                r = t // nq
                q0 = pl.multiple_of((t % nq) * tq, tq)
                start = pl.multiple_of(jnp.clip(q0 - A_HALF, 0, L - W), A_HALF)
                mask = jnp.abs(col_minus_row + (start - q0)) <= A_HALF
                tok = pl.ds(q0, tq) if dil == 1 else pl.ds(q0 * dil + r, tq, stride=dil)
                qt = q_ref[r, pl.ds(q0, tq), :]
                kw = k_ref[r, pl.ds(start, W), :]
                for h in range(2):
                    sl = slice(h * HEAD_DIM, (h + 1) * HEAD_DIM)
                    s = lax.dot_general(qt[:, sl], kw[:, sl], dn_nt, preferred_element_type=F32)
                    if mode == "max":
                        mrow = jnp.max(jnp.where(mask, s, -jnp.inf), axis=-1, keepdims=True)
                        mrow = jnp.broadcast_to(mrow, (tq, LANES))
                        m_scr[h, tok, :] = mrow if g == 0 else jnp.maximum(m_scr[h, tok, :], mrow)
                    else:
                        shift = bound if mode == "bound" else m_scr[h, tok, :][:, 0:1]
                        p = jnp.where(mask, jnp.exp2(s - shift), 0.0).astype(BF16)
                        res = jnp.dot(p, v_ref[h, r, pl.ds(start, W), :], preferred_element_type=F32)
                        acc_scr[h, tok, :] = res if g == 0 else acc_scr[h, tok, :] + res
                return carry

            lax.fori_loop(0, dil * nq, tile, 0)

    @pl.when(bound <= SAFE_SHIFT)
    def _():
        sweep("bound")

    @pl.when(bound > SAFE_SHIFT)
    def _():
        sweep("max")
        sweep("exact")

    fin = min(A_FIN, S)
    lane = lax.broadcasted_iota(jnp.int32, (fin, LANES), 1)

    def finish(c, carry):
        rows = pl.ds(pl.multiple_of(c * fin, fin), fin)
        x0, x1 = acc_scr[0, rows, :], acc_scr[1, rows, :]
        o0 = x0 / pltpu.roll(x0, HEAD_DIM, 1)
        o1 = pltpu.roll(x1, HEAD_DIM, 1) / x1
        o_ref[rows, :] = jnp.where(lane < HEAD_DIM, o0, o1).astype(BF16)
        return carry

    lax.fori_loop(0, S // fin, finish, 0)


def mixer_a(shift, a_qkv, Bn, S):
    in_specs = [pl.BlockSpec(memory_space=pltpu.SMEM)]
    for _, dil in A_PATTERNS:
        L = S // dil
        in_specs += [pl.BlockSpec((None, None, dil, L, LANES), lambda b, hp: (b, hp, 0, 0, 0)),
                     pl.BlockSpec((None, None, dil, L, LANES), lambda b, hp: (b, hp, 0, 0, 0)),
                     pl.BlockSpec((None, 2, dil, L, LANES), lambda b, hp: (b, hp, 0, 0, 0))]
    o = pl.pallas_call(
        functools.partial(_mixa_kernel, S=S),
        grid=(Bn, A_PAIRS),
        in_specs=in_specs,
        out_specs=pl.BlockSpec((None, S, LANES), lambda b, hp: (b, 0, hp)),
        out_shape=jax.ShapeDtypeStruct((Bn, S, A_WIDTH), BF16),
        scratch_shapes=[pltpu.VMEM((2, S, LANES), F32), pltpu.VMEM((2, S, LANES), F32)],
        compiler_params=_cparams(("arbitrary", "arbitrary")),
        name="mixer_a",
    )(shift, *a_qkv)
    return o.reshape(Bn * S, A_WIDTH)


SAFE_SHIFT = 56.0


def _score_bound(gq, gk):
    b = math.log2(math.e) * math.sqrt(HEAD_DIM) * jnp.max(jnp.abs(gq)) * jnp.max(jnp.abs(gk)) * 1.02
    return b.reshape(1).astype(F32)


def _mixc_kernel(shift_ref, q_ref, k_ref, v_ref, lam_ref, laminit_ref, g_ref, o_ref,
                 vaug_scr, c_scr, p1_scr, p2_scr, *, tq, tk, S):
    nk = S // tk
    dn_nt = (((1,), (1,)), ((), ()))
    maps = (slice(0, C_QK_DIM), slice(C_QK_DIM, 2 * C_QK_DIM))

    @pl.when(pl.program_id(2) == 0)
    def _():
        vaug_scr[:, :C_V_DIM] = v_ref[...]
        vaug_scr[:, C_V_DIM:] = jnp.ones((S, C_V_DIM), BF16)

    q = q_ref[...]
    bound = shift_ref[0]
    c_scr[...] = jnp.full((tq, LANES), bound, F32)

    @pl.when(bound > SAFE_SHIFT)
    def _():
        for sl in maps:
            m = jnp.full((tq, 1), -jnp.inf, F32)
            for j in range(nk):
                s = lax.dot_general(q[:, sl], k_ref[j * tk:(j + 1) * tk, sl], dn_nt, preferred_element_type=F32)
                m = jnp.maximum(m, jnp.max(s, axis=-1, keepdims=True))
            c_scr[:, sl] = jnp.broadcast_to(m, (tq, C_QK_DIM))

    for sl, p_scr in zip(maps, (p1_scr, p2_scr)):
        shift = c_scr[:, sl.start:sl.start + 1]
        for j in range(nk):
            s = lax.dot_general(q[:, sl], k_ref[j * tk:(j + 1) * tk, sl], dn_nt, preferred_element_type=F32)
            p_scr[:, j * tk:(j + 1) * tk] = jnp.exp2(s - shift).astype(BF16)
    a1 = jnp.dot(p1_scr[...], vaug_scr[...], preferred_element_type=F32)
    a2 = jnp.dot(p2_scr[...], vaug_scr[...], preferred_element_type=F32)
    lm = lam_ref[...]
    lam = (jnp.exp(jnp.sum(lm[0:1] * lm[1:2], axis=-1, keepdims=True))
           - jnp.exp(jnp.sum(lm[2:3] * lm[3:4], axis=-1, keepdims=True)) + laminit_ref[...])
    o = a1[:, :C_V_DIM] / a1[:, C_V_DIM:] - lam * (a2[:, :C_V_DIM] / a2[:, C_V_DIM:])
    ms = jnp.mean(o * o, axis=-1, keepdims=True)
    o_ref[...] = ((o * lax.rsqrt(ms + NORM_EPS)) * g_ref[...]).astype(BF16)


def mixer_c(qc, kc, vc, shift, c_lambda, lam_init, g_eff, Bn, S, tq=512, tk=512):
    tq, tk = min(tq, S), min(tk, S)
    q3, k3, v3 = (t.reshape(Bn, S, C_QK_WIDTH) for t in (qc, kc, vc))
    const = lambda shp: pl.BlockSpec(shp, lambda b, h, i: (0, 0))
    o = pl.pallas_call(
        functools.partial(_mixc_kernel, tq=tq, tk=tk, S=S),
        grid=(Bn, C_HEADS, S // tq),
        in_specs=[pl.BlockSpec(memory_space=pltpu.SMEM),
                  pl.BlockSpec((None, tq, LANES), lambda b, h, i: (b, i, h)),
                  pl.BlockSpec((None, S, LANES), lambda b, h, i: (b, 0, h)),
                  pl.BlockSpec((None, S, LANES), lambda b, h, i: (b, 0, h)),
                  const((4, C_QK_DIM)), const((1, 1)), const((1, LANES))],
        out_specs=pl.BlockSpec((None, tq, LANES), lambda b, h, i: (b, i, h)),
        out_shape=jax.ShapeDtypeStruct((Bn, S, C_WIDTH), BF16),
        scratch_shapes=[pltpu.VMEM((S, 2 * C_V_DIM), BF16), pltpu.VMEM((tq, LANES), F32),
                        pltpu.VMEM((tq, S), BF16), pltpu.VMEM((tq, S), BF16)],
        compiler_params=_cparams(("arbitrary", "arbitrary", "arbitrary")),
        name="mixer_c",
    )(shift, q3, k3, v3, c_lambda, lam_init, g_eff)
    return o.reshape(Bn * S, C_WIDTH)


def _bprep_kernel(c_ref, cp_ref, cn_ref, cw_ref, cb_ref, wq_ref, wk_ref, q_ref, k_ref, *, ts, nt):
    i = pl.program_id(1)
    c = c_ref[...]
    row = lax.broadcasted_iota(jnp.int32, (ts, 1), 0)
    prev_row = cp_ref[7:8, :] * jnp.where(i > 0, 1.0, 0.0)
    next_row = cn_ref[0:1, :] * jnp.where(i < nt - 1, 1.0, 0.0)
    c_prev = jnp.where(row == 0, prev_row, pltpu.roll(c, 1, 0))
    c_next = jnp.where(row == ts - 1, next_row, pltpu.roll(c, ts - 1, 0))
    cw = cw_ref[...]
    conv = cb_ref[...] + c_prev * cw[0:1] + c * cw[1:2] + c_next * cw[2:3]
    u = (conv * jax.nn.sigmoid(conv)).astype(BF16)
    kscale = 1.0 / math.sqrt(B_HEAD_DIM)
    for h in range(B_HEADS):
        sl = slice(h * B_HEAD_DIM, (h + 1) * B_HEAD_DIM)
        q_ref[:, sl] = jnp.dot(u[:, sl], wq_ref[h], preferred_element_type=F32).astype(BF16)
        k_ref[:, sl] = jnp.dot(u[:, sl], wk_ref[h], preferred_element_type=F32) * kscale


def mixer_b_prep(proj3, conv_w, conv_b, wq, wk, Bn, S, ts=512):
    nt = S // ts
    r8 = ts // 8
    nb8 = S // 8
    return pl.pallas_call(
        functools.partial(_bprep_kernel, ts=ts, nt=nt),
        grid=(Bn, nt),
        in_specs=[pl.BlockSpec((None, ts, B_WIDTH), lambda b, i: (b, i, COL_BC // B_WIDTH)),
                  pl.BlockSpec((None, 8, B_WIDTH), lambda b, i: (b, jnp.maximum(i * r8 - 1, 0), COL_BC // B_WIDTH)),
                  pl.BlockSpec((None, 8, B_WIDTH),
                               lambda b, i: (b, jnp.minimum((i + 1) * r8, nb8 - 1), COL_BC // B_WIDTH)),
                  pl.BlockSpec((3, B_WIDTH), lambda b, i: (0, 0)),
                  pl.BlockSpec((1, B_WIDTH), lambda b, i: (0, 0)),
                  pl.BlockSpec((B_HEADS, B_HEAD_DIM, B_HEAD_DIM), lambda b, i: (0, 0, 0)),
                  pl.BlockSpec((B_HEADS, B_HEAD_DIM, B_HEAD_DIM), lambda b, i: (0, 0, 0))],
        out_specs=[pl.BlockSpec((None, ts, B_WIDTH), lambda b, i: (b, i, 0))] * 2,
        out_shape=[jax.ShapeDtypeStruct((Bn, S, B_WIDTH), BF16), jax.ShapeDtypeStruct((Bn, S, B_WIDTH), F32)],
        compiler_params=_cparams(("arbitrary", "arbitrary")),
        name="mixer_b_prep",
    )(proj3, proj3, proj3, conv_w, conv_b, wq, wk)


def _split3(x):
    h1 = x.astype(BF16)
    r1 = x - h1.astype(F32)
    h2 = r1.astype(BF16)
    h3 = (r1 - h2.astype(F32)).astype(BF16)
    return h1, h2, h3


def _mlstm_chunk(q, k, v, li_col, li_row, b_col, b_row, b_last, causal_mask, C, n, m):
    logD = jnp.where(causal_mask, b_col - b_row + li_row, -jnp.inf)
    inter = b_col + m
    m_t = jnp.maximum(inter, jnp.max(logD, axis=-1, keepdims=True))
    qk = lax.dot_general(q, k.astype(BF16), (((1,), (1,)), ((), ())), preferred_element_type=F32)
    sc = qk * jnp.exp(logD - m_t)
    w_inter = jnp.exp(inter - m_t)
    num = (w_inter * jnp.dot(q, C.astype(BF16), preferred_element_type=F32)
           + jnp.dot(sc.astype(BF16), v.astype(BF16), preferred_element_type=F32))
    qn = jnp.sum(q.astype(F32) * n, axis=-1, keepdims=True)
    den = w_inter * qn + jnp.sum(sc, axis=-1, keepdims=True)
    h = num / jnp.maximum(jnp.abs(den), jnp.exp(-m_t))
    m_new = jnp.maximum(b_last + m, jnp.max(b_last - b_row + li_row, axis=-1, keepdims=True))
    decay = jnp.exp(b_last + m - m_new)
    wk = jnp.exp(b_last - b_col + li_col - m_new) * k
    C_new = decay * C + lax.dot_general(wk.astype(BF16), v.astype(BF16), (((0,), (0,)), ((), ())),
                                        preferred_element_type=F32)
    n_new = decay * n + jnp.sum(wk, axis=0, keepdims=True)
    return h, C_new, n_new, m_new


def _mlstm_kernel(qf_ref, kf_ref, vf_ref, gf_ref, qb_ref, kb_ref, vb_ref, gb_ref, bias_ref,
                  hf_ref, hb_ref, c_scr, n_scr, m_scr, *, ts):
    L = MLSTM_CHUNK
    nch = ts // L

    @pl.when(pl.program_id(1) == 0)
    def _():
        c_scr[...] = jnp.zeros_like(c_scr)
        n_scr[...] = jnp.zeros_like(n_scr)
        m_scr[...] = jnp.zeros_like(m_scr)

    row = lax.broadcasted_iota(jnp.int32, (L, L), 0)
    col = lax.broadcasted_iota(jnp.int32, (L, L), 1)
    lower = (col <= row)
    upper = (col >= row)
    tri_f = jnp.where(lower, 1.0, 0.0).astype(BF16)
    tri_b = jnp.where(upper, 1.0, 0.0).astype(BF16)
    bias = bias_ref[...]

    def direction(d, q_ref, k_ref, v_ref, g_ref, h_ref, tri, mask, chunk_order, last_row):
        for cidx in chunk_order:
            rows = slice(cidx * L, (cidx + 1) * L)
            g = g_ref[rows, :] + bias
            lf = -(jnp.maximum(-g, 0.0) + jnp.log1p(jnp.exp(-jnp.abs(g))))
            s1, s2, s3 = _split3(lf)
            cum = (jnp.dot(tri, s1, preferred_element_type=F32) + jnp.dot(tri, s2, preferred_element_type=F32)
                   + jnp.dot(tri, s3, preferred_element_type=F32))
            g_t = g.T
            cum_t = cum.T
            for h in range(B_HEADS):
                ci = (2 * d) * B_HEADS + h
                cf = (2 * d + 1) * B_HEADS + h
                sl = slice(h * B_HEAD_DIM, (h + 1) * B_HEAD_DIM)
                b_col = cum[:, cf:cf + 1]
                hh, c_new, n_new, m_new = _mlstm_chunk(
                    q_ref[rows, sl], k_ref[rows, sl], v_ref[rows, sl],
                    g[:, ci:ci + 1], g_t[ci:ci + 1, :], b_col, cum_t[cf:cf + 1, :],
                    b_col[last_row:last_row + 1, :], mask,
                    c_scr[d, h], n_scr[d, h], m_scr[d, h])
                h_ref[rows, sl] = hh
                c_scr[d, h] = c_new
                n_scr[d, h] = n_new
                m_scr[d, h] = m_new

    direction(0, qf_ref, kf_ref, vf_ref, gf_ref, hf_ref, tri_f, lower, range(nch), L - 1)
    direction(1, qb_ref, kb_ref, vb_ref, gb_ref, hb_ref, tri_b, upper, range(nch - 1, -1, -1), 0)


def mixer_b(qb, kb, proj3, gate_bias, Bn, S, ts=256):
    nt = S // ts
    fwd = lambda blk: (lambda b, i: (b, i, blk))
    bwd = lambda blk: (lambda b, i: (b, nt - 1 - i, blk))

    def specs(idx):
        return [pl.BlockSpec((None, ts, B_WIDTH), idx(0)),
                pl.BlockSpec((None, ts, B_WIDTH), idx(0)),
                pl.BlockSpec((None, ts, B_WIDTH), idx(COL_BV // B_WIDTH)),
                pl.BlockSpec((None, ts, LANES), idx(COL_BG // LANES))]

    return pl.pallas_call(
        functools.partial(_mlstm_kernel, ts=ts),
        grid=(Bn, nt),
        in_specs=specs(fwd) + specs(bwd) + [pl.BlockSpec((1, LANES), lambda b, i: (0, 0))],
        out_specs=[pl.BlockSpec((None, ts, B_WIDTH), fwd(0)), pl.BlockSpec((None, ts, B_WIDTH), bwd(0))],
        out_shape=[jax.ShapeDtypeStruct((Bn, S, B_WIDTH), F32)] * 2,
        scratch_shapes=[pltpu.VMEM((2, B_HEADS, B_HEAD_DIM, B_HEAD_DIM), F32),
                        pltpu.VMEM((2, B_HEADS, 1, B_HEAD_DIM), F32),
                        pltpu.VMEM((2, B_HEADS, 1, 1), F32)],
        compiler_params=_cparams(("arbitrary", "arbitrary")),
        name="mixer_b",
    )(qb, kb, proj3, proj3, qb, kb, proj3, proj3, gate_bias)


def _merge_kernel(x_ref, g0_ref, g1_ref, g2_ref, oa_ref, hf_ref, hb_ref, bo_ref, oc_ref, hng_ref,
                  wa_ref, wb_ref, wc_ref, wo_ref, y_ref):
    y_a = jnp.dot(oa_ref[...], wa_ref[...], preferred_element_type=F32)
    hsum = hf_ref[...] + hb_ref[...]
    hng = hng_ref[...]
    bo = bo_ref[...]
    parts = []
    for h in range(B_HEADS):
        sl = slice(h * B_HEAD_DIM, (h + 1) * B_HEAD_DIM)
        hh = hsum[:, sl]
        ms = jnp.mean(hh * hh, axis=-1, keepdims=True)
        parts.append(jax.nn.sigmoid(bo[:, sl]) * ((hh * lax.rsqrt(ms + NORM_EPS)) * hng[:, sl]))
    o_b = jnp.concatenate(parts, axis=-1)
    y_b = jnp.dot(o_b.astype(BF16), wb_ref[...], preferred_element_type=F32)
    y_c = jnp.dot(oc_ref[...], wc_ref[...], preferred_element_type=F32)
    mixed = (jax.nn.sigmoid(g0_ref[...]) * y_a + jax.nn.sigmoid(g1_ref[...]) * y_b
             + jax.nn.sigmoid(g2_ref[...]) * y_c)
    y_ref[...] = x_ref[...] + jnp.dot(mixed.astype(BF16), wo_ref[...], preferred_element_type=F32)


def merge(x2d, proj, oa, hf, hb, oc, hng, wa, wb, wc, wo, tm=256):
    T = x2d.shape[0]
    row = lambda w, blk=0: pl.BlockSpec((tm, w), lambda i: (i, blk))
    full = lambda a: pl.BlockSpec(a.shape, lambda i: (0, 0))
    gate = lambda j: row(D_MODEL, COL_GATE // D_MODEL + j)
    return pl.pallas_call(
        _merge_kernel,
        grid=(T // tm,),
        in_specs=[row(D_MODEL), gate(0), gate(1), gate(2), row(A_WIDTH),
                  row(B_WIDTH), row(B_WIDTH), row(B_WIDTH, COL_BO // B_WIDTH), row(C_WIDTH),
                  full(hng), full(wa), full(wb), full(wc), full(wo)],
        out_specs=row(D_MODEL),
        out_shape=jax.ShapeDtypeStruct((T, D_MODEL), F32),
        compiler_params=_cparams(("arbitrary",)),
        name="merge",
    )(x2d, proj, proj, proj, oa, hf, hb, proj, oc, hng, wa, wb, wc, wo)


FFN_CHUNK = D_FF // 2


def _ffn_kernel(x_ref, g_ref, wg_ref, wu_ref, wd_ref, y_ref):
    x = x_ref[...]
    ms = jnp.mean(x * x, axis=-1, keepdims=True)
    xn = ((x * lax.rsqrt(ms + NORM_EPS)) * g_ref[...]).astype(BF16)
    acc = x
    for c in range(D_FF // FFN_CHUNK):
        sl = slice(c * FFN_CHUNK, (c + 1) * FFN_CHUNK)
        hg = jnp.dot(xn, wg_ref[:, sl], preferred_element_type=F32)
        hu = jnp.dot(xn, wu_ref[:, sl], preferred_element_type=F32)
        a = (hg * jax.nn.sigmoid(hg)) * hu
        acc = acc + jnp.dot(a.astype(BF16), wd_ref[sl, :], preferred_element_type=F32)
    y_ref[...] = acc


def ffn(x2d, g, wg, wu, wd, tm=256):
    T = x2d.shape[0]
    full = lambda a: pl.BlockSpec(a.shape, lambda i: (0, 0))
    return pl.pallas_call(
        _ffn_kernel,
        grid=(T // tm,),
        in_specs=[pl.BlockSpec((tm, D_MODEL), lambda i: (i, 0)), full(g), full(wg), full(wu), full(wd)],
        out_specs=pl.BlockSpec((tm, D_MODEL), lambda i: (i, 0)),
        out_shape=jax.ShapeDtypeStruct((T, D_MODEL), F32),
        compiler_params=_cparams(("arbitrary",)),
        name="ffn",
    )(x2d, g, wg, wu, wd)


def _rotary_tables(S):
    inv = ROPE_THETA ** (-jnp.arange(0, ROT_DIM, 2, dtype=F32) / ROT_DIM)
    ang = jnp.arange(S, dtype=F32)[:, None] * inv[None, :]
    cos, sin = jnp.cos(ang), jnp.sin(ang)
    half = ROT_DIM // 2
    rest = HEAD_DIM - ROT_DIM
    one, zero, z8 = jnp.ones((S, rest), F32), jnp.zeros((S, rest), F32), jnp.zeros((S, half), F32)
    c64 = jnp.concatenate([cos, cos, one], axis=1)
    lo64 = jnp.concatenate([-sin, z8, zero], axis=1)
    hi64 = jnp.concatenate([z8, sin, zero], axis=1)
    return tuple(jnp.tile(t, (1, LANES // HEAD_DIM)) for t in (c64, lo64, hi64))


def _pack_layer(norm1_g, w_in, a_qn_g, a_kn_g, b_conv_w, b_conv_b, b_wq, b_wk, b_gate_bias, b_hn_g,
                c_qn_g, c_kn_g, c_lambda, c_hn_g, w_branch, w_out, norm2_g, w_ffn_in, w_ffn_out, layer_idx):
    o_bc = 3 * A_QKV
    o_bg = o_bc + 3 * B_WIDTH
    o_cq = o_bg + B_GATES
    o_gate = o_cq + 3 * C_QK_WIDTH
    w_re = jnp.concatenate([w_in[:, :o_bc], w_in[:, o_bg:o_cq], jnp.zeros((D_MODEL, LANES - B_GATES), F32),
                            w_in[:, o_bc:o_bg], w_in[:, o_gate:], w_in[:, o_cq:o_gate]], axis=1).astype(BF16)
    lam_init = 0.8 - 0.6 * math.exp(-0.3 * layer_idx)
    tile2 = lambda g: jnp.tile(g.reshape(1, HEAD_DIM), (1, LANES // HEAD_DIM))
    return dict(
        norm1_g=norm1_g.reshape(1, D_MODEL), w_in=w_re,
        gains=(tile2(a_qn_g), tile2(a_kn_g), tile2(c_qn_g), tile2(c_kn_g)),
        conv_w=b_conv_w, conv_b=b_conv_b.reshape(1, B_WIDTH), wq=b_wq.astype(BF16), wk=b_wk.astype(BF16),
        gate_bias=jnp.concatenate([b_gate_bias.reshape(1, B_GATES), jnp.zeros((1, LANES - B_GATES), F32)], axis=1),
        hn_g=b_hn_g.reshape(1, B_WIDTH),
        a_shift=_score_bound(a_qn_g, a_kn_g), c_shift=_score_bound(c_qn_g, c_kn_g),
        c_lambda=c_lambda, lam_init=jnp.full((1, 1), lam_init, F32),
        c_hn_g=(c_hn_g * (1.0 - lam_init)).reshape(1, C_V_DIM),
        wa=w_branch[:A_WIDTH].astype(BF16), wb=w_branch[A_WIDTH:A_WIDTH + B_WIDTH].astype(BF16),
        wc=w_branch[A_WIDTH + B_WIDTH:].astype(BF16), wo=w_out.astype(BF16),
        norm2_g=norm2_g.reshape(1, D_MODEL),
        wg=w_ffn_in[:, :D_FF].astype(BF16), wu=w_ffn_in[:, D_FF:].astype(BF16), wd=w_ffn_out.astype(BF16))


def _layer(x2d, p, tables, bd, Bn, S):
    proj = norm_matmul(x2d, p["norm1_g"], p["w_in"])
    *a_qkv, qc, kc, vc = qk_prep(proj, tables, bd, p["gains"], Bn, S)
    oa = mixer_a(p["a_shift"], a_qkv, Bn, S)
    proj3 = proj.reshape(Bn, S, PROJ_COLS)
    qb, kb = mixer_b_prep(proj3, p["conv_w"], p["conv_b"], p["wq"], p["wk"], Bn, S)
    hf, hb = mixer_b(qb, kb, proj3, p["gate_bias"], Bn, S)
    oc = mixer_c(qc, kc, vc, p["c_shift"], p["c_lambda"], p["lam_init"], p["c_hn_g"], Bn, S)
    x2d = merge(x2d, proj, oa, hf.reshape(Bn * S, B_WIDTH), hb.reshape(Bn * S, B_WIDTH), oc,
                p["hn_g"], p["wa"], p["wb"], p["wc"], p["wo"])
    return ffn(x2d, p["norm2_g"], p["wg"], p["wu"], p["wd"])


def _trunk(x, layers, bd):
    Bn, S, _ = x.shape
    tables = _rotary_tables(S)
    x2d = x.reshape(Bn * S, D_MODEL)
    for p in layers:
        x2d = _layer(x2d, p, tables, bd, Bn, S)
    return x2d.reshape(Bn, S, D_MODEL)


def kernel(x_prompt, x_sample, norm1_g, w_in, a_qn_g, a_kn_g, b_conv_w, b_conv_b, b_wq, b_wk, b_gate_bias, b_hn_g,
           c_qn_g, c_kn_g, c_lambda, c_hn_g, w_branch, w_out, norm2_g, w_ffn_in, w_ffn_out):
    weights = (norm1_g, w_in, a_qn_g, a_kn_g, b_conv_w, b_conv_b, b_wq, b_wk, b_gate_bias, b_hn_g,
               c_qn_g, c_kn_g, c_lambda, c_hn_g, w_branch, w_out, norm2_g, w_ffn_in, w_ffn_out)
    depth = norm1_g.shape[0]
    layers = [_pack_layer(*[w[l] for w in weights], layer_idx=l) for l in range(depth)]
    lane = jnp.arange(LANES) // HEAD_DIM
    bd = (lane[:, None] == lane[None, :]).astype(BF16)
    return (_trunk(x_prompt, layers, bd), _trunk(x_sample, layers, bd))
```

```python
import functools
import math

import jax
import jax.numpy as jnp
from jax import lax
from jax.experimental import pallas as pl
from jax.experimental.pallas import tpu as pltpu

F32 = jnp.float32
BF16 = jnp.bfloat16

D_MODEL = 1024
DEPTH = 4
NORM_EPS = 1e-6
HEAD_DIM = 64
ROT_DIM = HEAD_DIM // 4
ROPE_THETA = 500000.0

A_PATTERNS = ((128, 1), (512, 4), (2048, 16))
A_GROUPS = len(A_PATTERNS)
A_HEADS = 6
A_WIDTH = A_HEADS * HEAD_DIM
A_QKV = A_GROUPS * A_WIDTH

B_HEADS = 4
B_HEAD_DIM = 128
B_WIDTH = B_HEADS * B_HEAD_DIM
B_GATES = 4 * B_HEADS

C_HEADS = 4
C_QK_DIM = HEAD_DIM
C_V_DIM = 2 * C_QK_DIM
C_QK_WIDTH = C_HEADS * 2 * C_QK_DIM
C_WIDTH = C_HEADS * C_V_DIM

N_BRANCH = 3
D_FF = -(-8 * D_MODEL // (3 * 256)) * 256

LANES = 128
VMEM_LIMIT = 56 * 1024 * 1024

COL_AQ, COL_AK, COL_AV = 0, 1152, 2304
COL_BG = 3456
COL_BC, COL_BV, COL_BO = 3584, 4096, 4608
COL_CQ, COL_CK, COL_CV = 5120, 5632, 6144
PROJ_COLS = 6656
PROJ_TN = PROJ_COLS // 4

MLSTM_CHUNK = 128


def _cparams(sem):
    return pltpu.CompilerParams(dimension_semantics=sem, vmem_limit_bytes=VMEM_LIMIT)


def _norm_matmul_kernel(x_ref, g_ref, w_ref, o_ref):
    x = x_ref[...]
    ms = jnp.mean(x * x, axis=-1, keepdims=True)
    y = (x * lax.rsqrt(ms + NORM_EPS)) * g_ref[...]
    o_ref[...] = jnp.dot(y.astype(BF16), w_ref[...], preferred_element_type=F32)


def norm_matmul(x2d, g, w, tm=512, tn=PROJ_TN):
    T, K = x2d.shape
    N = w.shape[1]
    return pl.pallas_call(
        _norm_matmul_kernel,
        grid=(N // tn, T // tm),
        in_specs=[pl.BlockSpec((tm, K), lambda j, i: (i, 0)),
                  pl.BlockSpec((1, K), lambda j, i: (0, 0)),
                  pl.BlockSpec((K, tn), lambda j, i: (0, j))],
        out_specs=pl.BlockSpec((tm, tn), lambda j, i: (i, j)),
        out_shape=jax.ShapeDtypeStruct((T, N), F32),
        compiler_params=_cparams(("arbitrary", "arbitrary")),
        name="in_proj",
    )(x2d, g, w)


def _headnorm_rope(x, g, cos, s_lo, s_hi, bd, scale):
    x2 = x * x
    hi = x2.astype(BF16)
    lo = (x2 - hi.astype(F32)).astype(BF16)
    ss = jnp.dot(hi, bd, preferred_element_type=F32) + jnp.dot(lo, bd, preferred_element_type=F32)
    r = lax.rsqrt(ss * (1.0 / HEAD_DIM) + NORM_EPS)
    xg = x * g
    rot = (xg * cos + pltpu.roll(xg, LANES - ROT_DIM // 2, 1) * s_lo + pltpu.roll(xg, ROT_DIM // 2, 1) * s_hi)
    return rot * (r * scale)


A_PAIRS = A_HEADS // 2
QSCALE = math.log2(math.e) / math.sqrt(HEAD_DIM)


def _prep_kernel(aq_ref, ak_ref, av_ref, cq_ref, ck_ref, cv_ref, cos_ref, slo_ref, shi_ref, bd_ref,
                 gaq_ref, gak_ref, gcq_ref, gck_ref,
                 q0_ref, k0_ref, v0_ref, q1_ref, k1_ref, v1_ref, q2_ref, k2_ref, v2_ref,
                 ocq_ref, ock_ref, ocv_ref, relay_scr, *, tm):
    cos, s_lo, s_hi, bd = cos_ref[...], slo_ref[...], shi_ref[...], bd_ref[...]
    lane = lax.broadcasted_iota(jnp.int32, (tm, LANES), 1)

    def put(dst, idx, dil, y):
        if dil == 1:
            dst[idx, 0] = y.astype(BF16)
            return
        relay_scr[...] = y
        for r in range(dil):
            dst[idx, r] = relay_scr[pl.ds(r, tm // dil, stride=dil), :].astype(BF16)

    for src, dsts, g_ref, scale in ((aq_ref, (q0_ref, q1_ref, q2_ref), gaq_ref, QSCALE),
                                    (ak_ref, (k0_ref, k1_ref, k2_ref), gak_ref, 1.0)):
        g = g_ref[...]
        for c in range(A_QKV // LANES):
            grp, pair = divmod(c, A_PAIRS)
            y = _headnorm_rope(src[:, c * LANES:(c + 1) * LANES], g, cos, s_lo, s_hi, bd, scale)
            put(dsts[grp], pair, A_PATTERNS[grp][1], y)
    for c in range(A_QKV // LANES):
        grp, pair = divmod(c, A_PAIRS)
        y = av_ref[:, c * LANES:(c + 1) * LANES]
        dst = (v0_ref, v1_ref, v2_ref)[grp]
        put(dst, 2 * pair, A_PATTERNS[grp][1], jnp.where(lane < HEAD_DIM, y, 1.0))
        put(dst, 2 * pair + 1, A_PATTERNS[grp][1], jnp.where(lane < HEAD_DIM, pltpu.roll(y, HEAD_DIM, 1), 1.0))
    for src, dst, g_ref, scale in ((cq_ref, ocq_ref, gcq_ref, QSCALE), (ck_ref, ock_ref, gck_ref, 1.0)):
        g = g_ref[...]
        for c in range(C_QK_WIDTH // LANES):
            sl = slice(c * LANES, (c + 1) * LANES)
            dst[:, sl] = _headnorm_rope(src[:, sl], g, cos, s_lo, s_hi, bd, scale).astype(BF16)
    ocv_ref[...] = cv_ref[...].astype(BF16)


def qk_prep(proj, tables, bd, gains, Bn, S, tm=256):
    T = proj.shape[0]
    nps = S // tm

    def col(width, off):
        return pl.BlockSpec((tm, width), lambda i: (i, off // width))

    tab = pl.BlockSpec((tm, LANES), lambda i: (i % nps, 0))
    full = lambda shp: pl.BlockSpec(shp, lambda i: (0, 0))
    out_specs, out_shape = [], []
    for _, dil in A_PATTERNS:
        for n in (A_PAIRS, A_PAIRS, A_HEADS):
            out_specs.append(pl.BlockSpec((None, n, dil, tm // dil, LANES), lambda i: (i // nps, 0, 0, i % nps, 0)))
            out_shape.append(jax.ShapeDtypeStruct((Bn, n, dil, S // dil, LANES), BF16))
    for _ in range(3):
        out_specs.append(pl.BlockSpec((tm, C_QK_WIDTH), lambda i: (i, 0)))
        out_shape.append(jax.ShapeDtypeStruct((T, C_QK_WIDTH), BF16))
    return pl.pallas_call(
        functools.partial(_prep_kernel, tm=tm),
        grid=(T // tm,),
        in_specs=[col(A_QKV, COL_AQ), col(A_QKV, COL_AK), col(A_QKV, COL_AV),
                  col(C_QK_WIDTH, COL_CQ), col(C_QK_WIDTH, COL_CK), col(C_WIDTH, COL_CV),
                  tab, tab, tab, full((LANES, LANES))] + [full((1, LANES))] * 4,
        out_specs=out_specs,
        out_shape=out_shape,
        scratch_shapes=[pltpu.VMEM((tm, LANES), F32)],
        compiler_params=_cparams(("arbitrary",)),
        name="qk_prep",
    )(proj, proj, proj, proj, proj, proj, *tables, bd, *gains)


A_HALF = 64
assert all(w // (2 * d) == A_HALF for w, d in A_PATTERNS)
A_TQ = 128
A_UNROLL = 4
A_FIN = 512
A_MID = A_PATTERNS[1][1]
assert tuple(d for _, d in A_PATTERNS) == (1, A_MID, A_MID * A_MID)


def _mixa_kernel(shift_ref, q0_ref, k0_ref, v0_ref, q1_ref, k1_ref, v1_ref, q2_ref, k2_ref, v2_ref, o_ref,
                 acc_scr, accy_scr, m_scr, tmp_scr, *, S):
    qkv = ((q0_ref, k0_ref, v0_ref), (q1_ref, k1_ref, v1_ref), (q2_ref, k2_ref, v2_ref))
    bound = shift_ref[0]
    dn_nt = (((1,), (1,)), ((), ()))

    def sweep(mode):
        for g, (_, dil) in enumerate(A_PATTERNS):
            q_ref, k_ref, v_ref = qkv[g]
            L = S // dil
            tq = min(A_TQ, L)
            W = min(tq + 2 * A_HALF, L)
            nq = L // tq
            col_minus_row = (lax.broadcasted_iota(jnp.int32, (tq, W), 1)
                             - lax.broadcasted_iota(jnp.int32, (tq, W), 0))

            def tile(t, carry, g=g, dil=dil, L=L, tq=tq, W=W, nq=nq, q_ref=q_ref, k_ref=k_ref, v_ref=v_ref,
                     col_minus_row=col_minus_row):
                r = t // nq
                q0 = pl.multiple_of((t % nq) * tq, tq)
                start = pl.multiple_of(jnp.clip(q0 - A_HALF, 0, L - W), A_HALF)
                mask = jnp.abs(col_minus_row + (start - q0)) <= A_HALF
                tok = pl.ds(q0, tq) if dil == 1 else pl.ds(q0 * dil + r, tq, stride=dil)
                qt = q_ref[r, pl.ds(q0, tq), :]
                kw = k_ref[r, pl.ds(start, W), :]
                for h in range(2):
                    sl = slice(h * HEAD_DIM, (h + 1) * HEAD_DIM)
                    s = lax.dot_general(qt[:, sl], kw[:, sl], dn_nt, preferred_element_type=F32)
                    if mode == "max":
                        mrow = jnp.max(jnp.where(mask, s, -jnp.inf), axis=-1, keepdims=True)
                        mrow = jnp.broadcast_to(mrow, (tq, LANES))
                        m_scr[h, tok, :] = mrow if g == 0 else jnp.maximum(m_scr[h, tok, :], mrow)
                    else:
                        shift = bound if mode == "bound" else m_scr[h, tok, :][:, 0:1]
                        p = jnp.where(mask, jnp.exp2(s - shift), 0.0).astype(BF16)
                        res = jnp.dot(p, v_ref[h, r, pl.ds(start, W), :], preferred_element_type=F32)
                        if g == 0:
                            acc_scr[h, tok, :] = res
                        elif g == 1:
                            accy_scr[0, h, r, pl.ds(q0, tq), :] = res
                        else:
                            rows = pl.ds(q0 * A_MID + r // A_MID, tq, stride=A_MID)
                            accy_scr[1, h, r % A_MID, rows, :] = res
                return carry

            lax.fori_loop(0, dil * nq, tile, 0, unroll=min(A_UNROLL, dil * nq))

    @pl.when(bound <= SAFE_SHIFT)
    def _():
        sweep("bound")

    @pl.when(bound > SAFE_SHIFT)
    def _():
        sweep("max")
        sweep("exact")

    fin = min(A_FIN, S)
    lane = lax.broadcasted_iota(jnp.int32, (fin, LANES), 1)

    def finish(c, carry):
        rows = pl.ds(pl.multiple_of(c * fin, fin), fin)
        yrows = pl.ds(pl.multiple_of(c * (fin // A_MID), fin // A_MID), fin // A_MID)
        for h in range(2):
            for r in range(A_MID):
                tmp_scr[h, pl.ds(r, fin // A_MID, stride=A_MID), :] = (accy_scr[0, h, r, yrows, :]
                                                                       + accy_scr[1, h, r, yrows, :])
        x0, x1 = acc_scr[0, rows, :] + tmp_scr[0], acc_scr[1, rows, :] + tmp_scr[1]
        o0 = x0 / pltpu.roll(x0, HEAD_DIM, 1)
        o1 = pltpu.roll(x1, HEAD_DIM, 1) / x1
        o_ref[rows, :] = jnp.where(lane < HEAD_DIM, o0, o1).astype(BF16)
        return carry

    lax.fori_loop(0, S // fin, finish, 0)


def mixer_a(shift, a_qkv, Bn, S):
    in_specs = [pl.BlockSpec(memory_space=pltpu.SMEM)]
    for _, dil in A_PATTERNS:
        L = S // dil
        in_specs += [pl.BlockSpec((None, None, dil, L, LANES), lambda b, hp: (b, hp, 0, 0, 0)),
                     pl.BlockSpec((None, None, dil, L, LANES), lambda b, hp: (b, hp, 0, 0, 0)),
                     pl.BlockSpec((None, 2, dil, L, LANES), lambda b, hp: (b, hp, 0, 0, 0))]
    o = pl.pallas_call(
        functools.partial(_mixa_kernel, S=S),
        grid=(Bn, A_PAIRS),
        in_specs=in_specs,
        out_specs=pl.BlockSpec((None, S, LANES), lambda b, hp: (b, 0, hp)),
        out_shape=jax.ShapeDtypeStruct((Bn, S, A_WIDTH), BF16),
        scratch_shapes=[pltpu.VMEM((2, S, LANES), F32), pltpu.VMEM((2, 2, A_MID, S // A_MID, LANES), F32),
                        pltpu.VMEM((2, S, LANES), F32), pltpu.VMEM((2, min(A_FIN, S), LANES), F32)],
        compiler_params=_cparams(("arbitrary", "arbitrary")),
        name="mixer_a",
    )(shift, *a_qkv)
    return o.reshape(Bn * S, A_WIDTH)


SAFE_SHIFT = 56.0


def _score_bound(gq, gk):
    b = math.log2(math.e) * math.sqrt(HEAD_DIM) * jnp.max(jnp.abs(gq)) * jnp.max(jnp.abs(gk)) * 1.02
    return b.reshape(1).astype(F32)


def _mixc_kernel(shift_ref, q_ref, k_ref, v_ref, lam_ref, laminit_ref, g_ref, o_ref,
                 vaug_scr, c_scr, p1_scr, p2_scr, *, tq, tk, S):
    nk = S // tk
    dn_nt = (((1,), (1,)), ((), ()))
    maps = (slice(0, C_QK_DIM), slice(C_QK_DIM, 2 * C_QK_DIM))

    @pl.when(pl.program_id(2) == 0)
    def _():
        vaug_scr[:, :C_V_DIM] = v_ref[...]
        vaug_scr[:, C_V_DIM:] = jnp.ones((S, C_V_DIM), BF16)

    q = q_ref[...]
    bound = shift_ref[0]
    c_scr[...] = jnp.full((tq, LANES), bound, F32)

    @pl.when(bound > SAFE_SHIFT)
    def _():
        for sl in maps:
            m = jnp.full((tq, 1), -jnp.inf, F32)
            for j in range(nk):
                s = lax.dot_general(q[:, sl], k_ref[j * tk:(j + 1) * tk, sl], dn_nt, preferred_element_type=F32)
                m = jnp.maximum(m, jnp.max(s, axis=-1, keepdims=True))
            c_scr[:, sl] = jnp.broadcast_to(m, (tq, C_QK_DIM))

    for sl, p_scr in zip(maps, (p1_scr, p2_scr)):
        shift = c_scr[:, sl.start:sl.start + 1]
        for j in range(nk):
            s = lax.dot_general(q[:, sl], k_ref[j * tk:(j + 1) * tk, sl], dn_nt, preferred_element_type=F32)
            p_scr[:, j * tk:(j + 1) * tk] = jnp.exp2(s - shift).astype(BF16)
    a1 = jnp.dot(p1_scr[...], vaug_scr[...], preferred_element_type=F32)
    a2 = jnp.dot(p2_scr[...], vaug_scr[...], preferred_element_type=F32)
    lm = lam_ref[...]
    lam = (jnp.exp(jnp.sum(lm[0:1] * lm[1:2], axis=-1, keepdims=True))
           - jnp.exp(jnp.sum(lm[2:3] * lm[3:4], axis=-1, keepdims=True)) + laminit_ref[...])
    o = a1[:, :C_V_DIM] / a1[:, C_V_DIM:] - lam * (a2[:, :C_V_DIM] / a2[:, C_V_DIM:])
    ms = jnp.mean(o * o, axis=-1, keepdims=True)
    o_ref[...] = ((o * lax.rsqrt(ms + NORM_EPS)) * g_ref[...]).astype(BF16)


def mixer_c(qc, kc, vc, shift, c_lambda, lam_init, g_eff, Bn, S, tq=512, tk=512):
    tq, tk = min(tq, S), min(tk, S)
    q3, k3, v3 = (t.reshape(Bn, S, C_QK_WIDTH) for t in (qc, kc, vc))
    const = lambda shp: pl.BlockSpec(shp, lambda b, h, i: (0, 0))
    o = pl.pallas_call(
        functools.partial(_mixc_kernel, tq=tq, tk=tk, S=S),
        grid=(Bn, C_HEADS, S // tq),
        in_specs=[pl.BlockSpec(memory_space=pltpu.SMEM),
                  pl.BlockSpec((None, tq, LANES), lambda b, h, i: (b, i, h)),
                  pl.BlockSpec((None, S, LANES), lambda b, h, i: (b, 0, h)),
                  pl.BlockSpec((None, S, LANES), lambda b, h, i: (b, 0, h)),
                  const((4, C_QK_DIM)), const((1, 1)), const((1, LANES))],
        out_specs=pl.BlockSpec((None, tq, LANES), lambda b, h, i: (b, i, h)),
        out_shape=jax.ShapeDtypeStruct((Bn, S, C_WIDTH), BF16),
        scratch_shapes=[pltpu.VMEM((S, 2 * C_V_DIM), BF16), pltpu.VMEM((tq, LANES), F32),
                        pltpu.VMEM((tq, S), BF16), pltpu.VMEM((tq, S), BF16)],
        compiler_params=_cparams(("arbitrary", "arbitrary", "arbitrary")),
        name="mixer_c",
    )(shift, q3, k3, v3, c_lambda, lam_init, g_eff)
    return o.reshape(Bn * S, C_WIDTH)


def _bprep_kernel(c_ref, cp_ref, cn_ref, cw_ref, cb_ref, wq_ref, wkt_ref, q_ref, kt_ref, *, ts, nt):
    i = pl.program_id(1)
    c = c_ref[...]
    row = lax.broadcasted_iota(jnp.int32, (ts, 1), 0)
    prev_row = cp_ref[7:8, :] * jnp.where(i > 0, 1.0, 0.0)
    next_row = cn_ref[0:1, :] * jnp.where(i < nt - 1, 1.0, 0.0)
    c_prev = jnp.where(row == 0, prev_row, pltpu.roll(c, 1, 0))
    c_next = jnp.where(row == ts - 1, next_row, pltpu.roll(c, ts - 1, 0))
    cw = cw_ref[...]
    conv = cb_ref[...] + c_prev * cw[0:1] + c * cw[1:2] + c_next * cw[2:3]
    u = (conv * jax.nn.sigmoid(conv)).astype(BF16)
    kscale = 1.0 / math.sqrt(B_HEAD_DIM)
    for h in range(B_HEADS):
        sl = slice(h * B_HEAD_DIM, (h + 1) * B_HEAD_DIM)
        q_ref[:, sl] = jnp.dot(u[:, sl], wq_ref[h], preferred_element_type=F32).astype(BF16)
        kt_ref[h] = lax.dot_general(wkt_ref[h], u[:, sl], (((1,), (1,)), ((), ())),
                                    preferred_element_type=F32) * kscale


def mixer_b_prep(proj3, conv_w, conv_b, wq, wk, Bn, S, ts=512):
    nt = S // ts
    r8 = ts // 8
    nb8 = S // 8
    return pl.pallas_call(
        functools.partial(_bprep_kernel, ts=ts, nt=nt),
        grid=(Bn, nt),
        in_specs=[pl.BlockSpec((None, ts, B_WIDTH), lambda b, i: (b, i, COL_BC // B_WIDTH)),
                  pl.BlockSpec((None, 8, B_WIDTH), lambda b, i: (b, jnp.maximum(i * r8 - 1, 0), COL_BC // B_WIDTH)),
                  pl.BlockSpec((None, 8, B_WIDTH),
                               lambda b, i: (b, jnp.minimum((i + 1) * r8, nb8 - 1), COL_BC // B_WIDTH)),
                  pl.BlockSpec((3, B_WIDTH), lambda b, i: (0, 0)),
                  pl.BlockSpec((1, B_WIDTH), lambda b, i: (0, 0)),
                  pl.BlockSpec((B_HEADS, B_HEAD_DIM, B_HEAD_DIM), lambda b, i: (0, 0, 0)),
                  pl.BlockSpec((B_HEADS, B_HEAD_DIM, B_HEAD_DIM), lambda b, i: (0, 0, 0))],
        out_specs=[pl.BlockSpec((None, ts, B_WIDTH), lambda b, i: (b, i, 0)),
                   pl.BlockSpec((None, B_HEADS, B_HEAD_DIM, ts), lambda b, i: (b, 0, 0, i))],
        out_shape=[jax.ShapeDtypeStruct((Bn, S, B_WIDTH), BF16),
                   jax.ShapeDtypeStruct((Bn, B_HEADS, B_HEAD_DIM, S), F32)],
        compiler_params=_cparams(("arbitrary", "arbitrary")),
        name="mixer_b_prep",
    )(proj3, proj3, proj3, conv_w, conv_b, wq, wk)


def _split3(x):
    h1 = x.astype(BF16)
    r1 = x - h1.astype(F32)
    h2 = r1.astype(BF16)
    h3 = (r1 - h2.astype(F32)).astype(BF16)
    return h1, h2, h3


def _mlstm_kernel(qf_ref, ktf_ref, vf_ref, gf_ref, qb_ref, ktb_ref, vb_ref, gb_ref, bias_ref,
                  hf_ref, hb_ref, c_scr, m_scr, *, ts):
    L = MLSTM_CHUNK
    D = B_HEAD_DIM
    nch = ts // L

    @pl.when(pl.program_id(1) == 0)
    def _():
        c_scr[...] = jnp.zeros_like(c_scr)
        m_scr[...] = jnp.zeros_like(m_scr)

    row = lax.broadcasted_iota(jnp.int32, (L, L), 0)
    col = lax.broadcasted_iota(jnp.int32, (L, L), 1)
    bias = bias_ref[...]
    ones = jnp.ones((L, D), BF16)

    dirs = ((qf_ref, ktf_ref, vf_ref, gf_ref, hf_ref), (qb_ref, ktb_ref, vb_ref, gb_ref, hb_ref))
    masks = (col <= row, col >= row)
    lasts = (L - 1, 0)

    pre = {}
    for d in range(2):
        g_ref = dirs[d][3]
        tri = jnp.where(masks[d], 1.0, 0.0).astype(BF16)
        for cidx in range(nch):
            g = g_ref[cidx * L:(cidx + 1) * L, :] + bias
            lf = -(jnp.maximum(-g, 0.0) + jnp.log1p(jnp.exp(-jnp.abs(g))))
            s1, s2, s3 = _split3(lf)
            bsum = (jnp.dot(tri, s1, preferred_element_type=F32) + jnp.dot(tri, s2, preferred_element_type=F32)
                    + jnp.dot(tri, s3, preferred_element_type=F32))
            bsum = pltpu.roll(bsum, LANES - B_HEADS, 1)
            a_rows = (g - bsum).T
            for h in range(B_HEADS):
                j = 2 * B_HEADS * d + h
                a_row = a_rows[j:j + 1, :]
                a_max = jnp.max(jnp.where(masks[d], a_row, -jnp.inf), axis=1, keepdims=True)
                pre[d, cidx, h] = (a_row, bsum[:, j:j + 1], a_max)

    state = {(d, h): (c_scr[d, h], m_scr[d, h]) for d in range(2) for h in range(B_HEADS)}
    for step in range(nch):
        for d in range(2):
            q_ref, kt_ref, v_ref, _, h_ref = dirs[d]
            cidx = nch - 1 - step if d == 1 else step
            rows = slice(cidx * L, (cidx + 1) * L)
            last = lasts[d]
            for h in range(B_HEADS):
                sl = slice(h * D, (h + 1) * D)
                a_row, b_col, a_max = pre[d, cidx, h]
                caug, m_old = state[d, h]
                big_m = jnp.maximum(m_old, a_max)
                m_last = big_m[last:last + 1, :]
                q, kt = q_ref[rows, sl], kt_ref[h, :, rows]
                vaug = jnp.concatenate([v_ref[rows, sl].astype(BF16), ones], axis=1)
                w_intra = jnp.where(masks[d], jnp.exp(a_row - big_m), 0.0)
                sc = jnp.dot(q, kt.astype(BF16), preferred_element_type=F32) * w_intra
                na = (jnp.exp(m_old - big_m) * jnp.dot(q, caug.astype(BF16), preferred_element_type=F32)
                      + jnp.dot(sc.astype(BF16), vaug, preferred_element_type=F32))
                h_ref[rows, sl] = na[:, :D] / jnp.maximum(jnp.abs(na[:, D:]), jnp.exp(-(b_col + big_m)))
                wkt = (kt * jnp.exp(a_row - m_last)).astype(BF16)
                state[d, h] = (jnp.exp(m_old - m_last) * caug + jnp.dot(wkt, vaug, preferred_element_type=F32),
                               b_col[last:last + 1, :] + m_last)
    for (d, h), (caug, m) in state.items():
        c_scr[d, h] = caug
        m_scr[d, h] = m


def mixer_b(qb, ktb, proj3, gate_bias, Bn, S, ts=256):
    nt = S // ts
    fwd = lambda blk: (lambda b, i: (b, i, blk))
    bwd = lambda blk: (lambda b, i: (b, nt - 1 - i, blk))

    def specs(idx):
        seq = idx(0)
        return [pl.BlockSpec((None, ts, B_WIDTH), seq),
                pl.BlockSpec((None, B_HEADS, B_HEAD_DIM, ts), lambda b, i: (b, 0, 0, seq(b, i)[1])),
                pl.BlockSpec((None, ts, B_WIDTH), idx(COL_BV // B_WIDTH)),
                pl.BlockSpec((None, ts, LANES), idx(COL_BG // LANES))]

    return pl.pallas_call(
        functools.partial(_mlstm_kernel, ts=ts),
        grid=(Bn, nt),
        in_specs=specs(fwd) + specs(bwd) + [pl.BlockSpec((1, LANES), lambda b, i: (0, 0))],
        out_specs=[pl.BlockSpec((None, ts, B_WIDTH), fwd(0)), pl.BlockSpec((None, ts, B_WIDTH), bwd(0))],
        out_shape=[jax.ShapeDtypeStruct((Bn, S, B_WIDTH), F32)] * 2,
        scratch_shapes=[pltpu.VMEM((2, B_HEADS, B_HEAD_DIM, 2 * B_HEAD_DIM), F32),
                        pltpu.VMEM((2, B_HEADS, 1, 1), F32)],
        compiler_params=_cparams(("arbitrary", "arbitrary")),
        name="mixer_b",
    )(qb, ktb, proj3, proj3, qb, ktb, proj3, proj3, gate_bias)


def _merge_kernel(x_ref, n1g_ref, wgate_ref, oa_ref, hf_ref, hb_ref, bo_ref, oc_ref, hng_ref,
                  wa_ref, wb_ref, wc_ref, wo_ref, y_ref):
    x = x_ref[...]
    ms = jnp.mean(x * x, axis=-1, keepdims=True)
    xn = ((x * lax.rsqrt(ms + NORM_EPS)) * n1g_ref[...]).astype(BF16)
    gates = [jax.nn.sigmoid(jnp.dot(xn, wgate_ref[:, j * D_MODEL:(j + 1) * D_MODEL], preferred_element_type=F32))
             for j in range(N_BRANCH)]
    y_a = jnp.dot(oa_ref[...], wa_ref[...], preferred_element_type=F32)
    hsum = hf_ref[...] + hb_ref[...]
    hng = hng_ref[...]
    bo = bo_ref[...]
    parts = []
    for h in range(B_HEADS):
        sl = slice(h * B_HEAD_DIM, (h + 1) * B_HEAD_DIM)
        hh = hsum[:, sl]
        ms = jnp.mean(hh * hh, axis=-1, keepdims=True)
        parts.append(jax.nn.sigmoid(bo[:, sl]) * ((hh * lax.rsqrt(ms + NORM_EPS)) * hng[:, sl]))
    o_b = jnp.concatenate(parts, axis=-1)
    y_b = jnp.dot(o_b.astype(BF16), wb_ref[...], preferred_element_type=F32)
    y_c = jnp.dot(oc_ref[...], wc_ref[...], preferred_element_type=F32)
    mixed = gates[0] * y_a + gates[1] * y_b + gates[2] * y_c
    y_ref[...] = x + jnp.dot(mixed.astype(BF16), wo_ref[...], preferred_element_type=F32)


def _resident(a):
    return pl.BlockSpec(a.shape, lambda i: (0,) * a.ndim, pipeline_mode=pl.Buffered(1))


def merge(x2d, n1g, wgate, proj, oa, hf, hb, oc, hng, wa, wb, wc, wo, tm=512):
    T = x2d.shape[0]
    row = lambda w, blk=0: pl.BlockSpec((tm, w), lambda i: (i, blk))
    return pl.pallas_call(
        _merge_kernel,
        grid=(T // tm,),
        in_specs=[row(D_MODEL), _resident(n1g), _resident(wgate), row(A_WIDTH),
                  row(B_WIDTH), row(B_WIDTH), row(B_WIDTH, COL_BO // B_WIDTH), row(C_WIDTH),
                  _resident(hng), _resident(wa), _resident(wb), _resident(wc), _resident(wo)],
        out_specs=row(D_MODEL),
        out_shape=jax.ShapeDtypeStruct((T, D_MODEL), F32),
        compiler_params=_cparams(("arbitrary",)),
        name="merge",
    )(x2d, n1g, wgate, oa, hf, hb, proj, oc, hng, wa, wb, wc, wo)


FFN_CHUNK = D_FF // 2


def _ffn_kernel(x_ref, g_ref, wg_ref, wu_ref, wd_ref, y_ref):
    x = x_ref[...]
    ms = jnp.mean(x * x, axis=-1, keepdims=True)
    xn = ((x * lax.rsqrt(ms + NORM_EPS)) * g_ref[...]).astype(BF16)
    acc = x
    for c in range(D_FF // FFN_CHUNK):
        sl = slice(c * FFN_CHUNK, (c + 1) * FFN_CHUNK)
        hg = jnp.dot(xn, wg_ref[:, sl], preferred_element_type=F32)
        hu = jnp.dot(xn, wu_ref[:, sl], preferred_element_type=F32)
        a = (hg * jax.nn.sigmoid(hg)) * hu
        acc = acc + jnp.dot(a.astype(BF16), wd_ref[sl, :], preferred_element_type=F32)
    y_ref[...] = acc


def ffn(x2d, g, wg, wu, wd, tm=512):
    T = x2d.shape[0]
    return pl.pallas_call(
        _ffn_kernel,
        grid=(T // tm,),
        in_specs=[pl.BlockSpec((tm, D_MODEL), lambda i: (i, 0)),
                  _resident(g), _resident(wg), _resident(wu), _resident(wd)],
        out_specs=pl.BlockSpec((tm, D_MODEL), lambda i: (i, 0)),
        out_shape=jax.ShapeDtypeStruct((T, D_MODEL), F32),
        compiler_params=_cparams(("arbitrary",)),
        name="ffn",
    )(x2d, g, wg, wu, wd)


def _rotary_tables(S):
    inv = ROPE_THETA ** (-jnp.arange(0, ROT_DIM, 2, dtype=F32) / ROT_DIM)
    ang = jnp.arange(S, dtype=F32)[:, None] * inv[None, :]
    cos, sin = jnp.cos(ang), jnp.sin(ang)
    half = ROT_DIM // 2
    rest = HEAD_DIM - ROT_DIM
    one, zero, z8 = jnp.ones((S, rest), F32), jnp.zeros((S, rest), F32), jnp.zeros((S, half), F32)
    c64 = jnp.concatenate([cos, cos, one], axis=1)
    lo64 = jnp.concatenate([-sin, z8, zero], axis=1)
    hi64 = jnp.concatenate([z8, sin, zero], axis=1)
    return tuple(jnp.tile(t, (1, LANES // HEAD_DIM)) for t in (c64, lo64, hi64))


def _pack_layer(norm1_g, w_in, a_qn_g, a_kn_g, b_conv_w, b_conv_b, b_wq, b_wk, b_gate_bias, b_hn_g,
                c_qn_g, c_kn_g, c_lambda, c_hn_g, w_branch, w_out, norm2_g, w_ffn_in, w_ffn_out, layer_idx):
    o_bc = 3 * A_QKV
    o_bg = o_bc + 3 * B_WIDTH
    o_cq = o_bg + B_GATES
    o_gate = o_cq + 3 * C_QK_WIDTH
    w_re = jnp.concatenate([w_in[:, :o_bc], w_in[:, o_bg:o_cq], jnp.zeros((D_MODEL, LANES - B_GATES), F32),
                            w_in[:, o_bc:o_bg], w_in[:, o_cq:o_gate]], axis=1).astype(BF16)
    lam_init = 0.8 - 0.6 * math.exp(-0.3 * layer_idx)
    tile2 = lambda g: jnp.tile(g.reshape(1, HEAD_DIM), (1, LANES // HEAD_DIM))
    return dict(
        norm1_g=norm1_g.reshape(1, D_MODEL), w_in=w_re, w_gate=w_in[:, o_gate:].astype(BF16),
        gains=(tile2(a_qn_g), tile2(a_kn_g), tile2(c_qn_g), tile2(c_kn_g)),
        conv_w=b_conv_w, conv_b=b_conv_b.reshape(1, B_WIDTH), wq=b_wq.astype(BF16),
        wkt=jnp.swapaxes(b_wk, 1, 2).astype(BF16),
        gate_bias=jnp.concatenate([b_gate_bias.reshape(1, B_GATES), jnp.zeros((1, LANES - B_GATES), F32)], axis=1),
        hn_g=b_hn_g.reshape(1, B_WIDTH),
        a_shift=_score_bound(a_qn_g, a_kn_g), c_shift=_score_bound(c_qn_g, c_kn_g),
        c_lambda=c_lambda, lam_init=jnp.full((1, 1), lam_init, F32),
        c_hn_g=(c_hn_g * (1.0 - lam_init)).reshape(1, C_V_DIM),
        wa=w_branch[:A_WIDTH].astype(BF16), wb=w_branch[A_WIDTH:A_WIDTH + B_WIDTH].astype(BF16),
        wc=w_branch[A_WIDTH + B_WIDTH:].astype(BF16), wo=w_out.astype(BF16),
        norm2_g=norm2_g.reshape(1, D_MODEL),
        wg=w_ffn_in[:, :D_FF].astype(BF16), wu=w_ffn_in[:, D_FF:].astype(BF16), wd=w_ffn_out.astype(BF16))


def _layer(x2d, p, tables, bd, Bn, S):
    proj = norm_matmul(x2d, p["norm1_g"], p["w_in"])
    *a_qkv, qc, kc, vc = qk_prep(proj, tables, bd, p["gains"], Bn, S)
    oa = mixer_a(p["a_shift"], a_qkv, Bn, S)
    proj3 = proj.reshape(Bn, S, PROJ_COLS)
    qb, ktb = mixer_b_prep(proj3, p["conv_w"], p["conv_b"], p["wq"], p["wkt"], Bn, S)
    hf, hb = mixer_b(qb, ktb, proj3, p["gate_bias"], Bn, S)
    oc = mixer_c(qc, kc, vc, p["c_shift"], p["c_lambda"], p["lam_init"], p["c_hn_g"], Bn, S)
    x2d = merge(x2d, p["norm1_g"], p["w_gate"], proj, oa,
                hf.reshape(Bn * S, B_WIDTH), hb.reshape(Bn * S, B_WIDTH), oc,
                p["hn_g"], p["wa"], p["wb"], p["wc"], p["wo"])
    return ffn(x2d, p["norm2_g"], p["wg"], p["wu"], p["wd"])


def _trunk(x, layers, bd):
    Bn, S, _ = x.shape
    tables = _rotary_tables(S)
    x2d = x.reshape(Bn * S, D_MODEL)
    for p in layers:
        x2d = _layer(x2d, p, tables, bd, Bn, S)
    return x2d.reshape(Bn, S, D_MODEL)


def kernel(x_prompt, x_sample, norm1_g, w_in, a_qn_g, a_kn_g, b_conv_w, b_conv_b, b_wq, b_wk, b_gate_bias, b_hn_g,
           c_qn_g, c_kn_g, c_lambda, c_hn_g, w_branch, w_out, norm2_g, w_ffn_in, w_ffn_out):
    weights = (norm1_g, w_in, a_qn_g, a_kn_g, b_conv_w, b_conv_b, b_wq, b_wk, b_gate_bias, b_hn_g,
               c_qn_g, c_kn_g, c_lambda, c_hn_g, w_branch, w_out, norm2_g, w_ffn_in, w_ffn_out)
    depth = norm1_g.shape[0]
    layers = [_pack_layer(*[w[l] for w in weights], layer_idx=l) for l in range(depth)]
    lane = jnp.arange(LANES) // HEAD_DIM
    bd = (lane[:, None] == lane[None, :]).astype(BF16)
    return (_trunk(x_prompt, layers, bd), _trunk(x_sample, layers, bd))
```

```python
import functools
import math

import jax
import jax.numpy as jnp
from jax import lax
from jax.experimental import pallas as pl
from jax.experimental.pallas import tpu as pltpu

F32 = jnp.float32
BF16 = jnp.bfloat16

D_MODEL = 1024
DEPTH = 4
NORM_EPS = 1e-6
HEAD_DIM = 64
ROT_DIM = HEAD_DIM // 4
ROPE_THETA = 500000.0

A_PATTERNS = ((128, 1), (512, 4), (2048, 16))
A_GROUPS = len(A_PATTERNS)
A_HEADS = 6
A_WIDTH = A_HEADS * HEAD_DIM
A_QKV = A_GROUPS * A_WIDTH

B_HEADS = 4
B_HEAD_DIM = 128
B_WIDTH = B_HEADS * B_HEAD_DIM
B_GATES = 4 * B_HEADS

C_HEADS = 4
C_QK_DIM = HEAD_DIM
C_V_DIM = 2 * C_QK_DIM
C_QK_WIDTH = C_HEADS * 2 * C_QK_DIM
C_WIDTH = C_HEADS * C_V_DIM

N_BRANCH = 3
D_FF = -(-8 * D_MODEL // (3 * 256)) * 256

LANES = 128
VMEM_LIMIT = 56 * 1024 * 1024

COL_BC, COL_BV, COL_BO = 0, 512, 1024
COL_BG = 1536
PROJ_COLS = 1664
PROJ_TN = PROJ_COLS

MLSTM_CHUNK = 128


def _cparams(sem):
    return pltpu.CompilerParams(dimension_semantics=sem, vmem_limit_bytes=VMEM_LIMIT)


def _norm_matmul_kernel(x_ref, g_ref, w_ref, o_ref):
    x = x_ref[...]
    ms = jnp.mean(x * x, axis=-1, keepdims=True)
    y = (x * lax.rsqrt(ms + NORM_EPS)) * g_ref[...]
    o_ref[...] = jnp.dot(y.astype(BF16), w_ref[...], preferred_element_type=F32)


def norm_matmul(x2d, g, w, tm=512, tn=PROJ_TN):
    T, K = x2d.shape
    N = w.shape[1]
    return pl.pallas_call(
        _norm_matmul_kernel,
        grid=(N // tn, T // tm),
        in_specs=[pl.BlockSpec((tm, K), lambda j, i: (i, 0)),
                  pl.BlockSpec((1, K), lambda j, i: (0, 0)),
                  pl.BlockSpec((K, tn), lambda j, i: (0, j))],
        out_specs=pl.BlockSpec((tm, tn), lambda j, i: (i, j)),
        out_shape=jax.ShapeDtypeStruct((T, N), F32),
        compiler_params=_cparams(("arbitrary", "arbitrary")),
        name="in_proj",
    )(x2d, g, w)


def _resident(a):
    return pl.BlockSpec(a.shape, lambda *_: (0,) * a.ndim, pipeline_mode=pl.Buffered(1))


A_PAIRS = A_HEADS // 2
QSCALE = math.log2(math.e) / math.sqrt(HEAD_DIM)
ATT_AQ, ATT_AK, ATT_AV = 0, A_QKV, 2 * A_QKV
ATT_CQ, ATT_CK, ATT_CV = 3 * A_QKV, 3 * A_QKV + C_QK_WIDTH, 3 * A_QKV + 2 * C_QK_WIDTH
ATT_COLS = 3 * A_QKV + 3 * C_QK_WIDTH
ATT_DOT = ATT_COLS // 3


def _headnorm_rope(x, ss, g, cos, s_lo, s_hi, scale):
    r = lax.rsqrt(ss * (1.0 / HEAD_DIM) + NORM_EPS)
    xg = x * g
    rot = (xg * cos + pltpu.roll(xg, LANES - ROT_DIM // 2, 1) * s_lo + pltpu.roll(xg, ROT_DIM // 2, 1) * s_hi)
    return rot * (r * scale)


def _attn_proj_kernel(x_ref, n1g_ref, w_ref, cos_ref, slo_ref, shi_ref, bd_ref,
                      gaq_ref, gak_ref, gcq_ref, gck_ref,
                      q0_ref, k0_ref, v0_ref, q1_ref, k1_ref, v1_ref, q2_ref, k2_ref, v2_ref,
                      ocq_ref, ock_ref, ocv_ref, proj_scr, relay_scr, *, tm):
    x = x_ref[...]
    ms = jnp.mean(x * x, axis=-1, keepdims=True)
    xn = ((x * lax.rsqrt(ms + NORM_EPS)) * n1g_ref[...]).astype(BF16)
    for j in range(ATT_COLS // ATT_DOT):
        cols = slice(j * ATT_DOT, (j + 1) * ATT_DOT)
        proj_scr[:, cols] = jnp.dot(xn, w_ref[:, cols], preferred_element_type=F32)

    cos, s_lo, s_hi, bd = cos_ref[...], slo_ref[...], shi_ref[...], bd_ref[...]
    lane = lax.broadcasted_iota(jnp.int32, (tm, LANES), 1)

    def put(dst, idx, dil, y):
        if dil == 1:
            dst[idx, 0] = y.astype(BF16)
            return
        relay_scr[...] = y
        for r in range(dil):
            dst[idx, r] = relay_scr[pl.ds(r, tm // dil, stride=dil), :].astype(BF16)

    def normed_pair(off, params):
        x2 = proj_scr[:, off:off + 2 * LANES]
        ss = jnp.dot((x2 * x2).astype(BF16), bd, preferred_element_type=F32)
        return [_headnorm_rope(x2[:, h * LANES:(h + 1) * LANES], ss[:, h * LANES:(h + 1) * LANES],
                               params[h][0], cos, s_lo, s_hi, params[h][1]) for h in range(2)]

    n_a = A_QKV // LANES
    gaq, gak = gaq_ref[...], gak_ref[...]
    for p in range(n_a):
        chunks = (2 * p, 2 * p + 1)
        params = [(gaq, QSCALE) if c < n_a else (gak, 1.0) for c in chunks]
        for c, y in zip(chunks, normed_pair(ATT_AQ + 2 * p * LANES, params)):
            dsts = (q0_ref, q1_ref, q2_ref) if c < n_a else (k0_ref, k1_ref, k2_ref)
            grp, pair = divmod(c % n_a, A_PAIRS)
            put(dsts[grp], pair, A_PATTERNS[grp][1], y)
    for c in range(n_a):
        grp, pair = divmod(c, A_PAIRS)
        y = proj_scr[:, ATT_AV + c * LANES:ATT_AV + (c + 1) * LANES]
        dst = (v0_ref, v1_ref, v2_ref)[grp]
        put(dst, 2 * pair, A_PATTERNS[grp][1], jnp.where(lane < HEAD_DIM, y, 1.0))
        put(dst, 2 * pair + 1, A_PATTERNS[grp][1], jnp.where(lane < HEAD_DIM, pltpu.roll(y, HEAD_DIM, 1), 1.0))
    for off, dst, g_ref, scale in ((ATT_CQ, ocq_ref, gcq_ref, QSCALE), (ATT_CK, ock_ref, gck_ref, 1.0)):
        g = g_ref[...]
        for p in range(C_QK_WIDTH // (2 * LANES)):
            for h, y in enumerate(normed_pair(off + 2 * p * LANES, [(g, scale)] * 2)):
                c = 2 * p + h
                dst[:, c * LANES:(c + 1) * LANES] = y.astype(BF16)
    ocv_ref[...] = proj_scr[:, ATT_CV:ATT_CV + C_WIDTH].astype(BF16)


def attn_proj(x2d, n1g, w_att, tables, bd, gains, Bn, S, tm=512):
    T = x2d.shape[0]
    tm = min(tm, S)
    nps = S // tm
    tab = pl.BlockSpec((tm, LANES), lambda i: (i % nps, 0))
    out_specs, out_shape = [], []
    for _, dil in A_PATTERNS:
        for n in (A_PAIRS, A_PAIRS, A_HEADS):
            out_specs.append(pl.BlockSpec((None, n, dil, tm // dil, LANES), lambda i: (i // nps, 0, 0, i % nps, 0)))
            out_shape.append(jax.ShapeDtypeStruct((Bn, n, dil, S // dil, LANES), BF16))
    for _ in range(3):
        out_specs.append(pl.BlockSpec((tm, C_QK_WIDTH), lambda i: (i, 0)))
        out_shape.append(jax.ShapeDtypeStruct((T, C_QK_WIDTH), BF16))
    return pl.pallas_call(
        functools.partial(_attn_proj_kernel, tm=tm),
        grid=(T // tm,),
        in_specs=[pl.BlockSpec((tm, D_MODEL), lambda i: (i, 0)), _resident(n1g), _resident(w_att),
                  tab, tab, tab, _resident(bd)] + [_resident(g) for g in gains],
        out_specs=out_specs,
        out_shape=out_shape,
        scratch_shapes=[pltpu.VMEM((tm, ATT_COLS), F32), pltpu.VMEM((tm, LANES), F32)],
        compiler_params=_cparams(("arbitrary",)),
        name="attn_proj",
    )(x2d, n1g, w_att, *tables, bd, *gains)


A_HALF = 64
assert all(w // (2 * d) == A_HALF for w, d in A_PATTERNS)
A_TQ = 128
A_UNROLL = 4
A_FIN = 512
A_MID = A_PATTERNS[1][1]
assert tuple(d for _, d in A_PATTERNS) == (1, A_MID, A_MID * A_MID)


def _mixa_kernel(shift_ref, q0_ref, k0_ref, v0_ref, q1_ref, k1_ref, v1_ref, q2_ref, k2_ref, v2_ref, o_ref,
                 acc_scr, accy_scr, m_scr, tmp_scr, *, S):
    qkv = ((q0_ref, k0_ref, v0_ref), (q1_ref, k1_ref, v1_ref), (q2_ref, k2_ref, v2_ref))
    bound = shift_ref[0]
    dn_nt = (((1,), (1,)), ((), ()))

    def tok_rows(dil, r, q0, n):
        return pl.ds(q0, n) if dil == 1 else pl.ds(q0 * dil + r, n, stride=dil)

    def put_result(g, h, r, q0, res):
        n = res.shape[0]
        if g == 0:
            acc_scr[h, pl.ds(q0, n), :] = res
        elif g == 1:
            accy_scr[0, h, r, pl.ds(q0, n), :] = res
        else:
            accy_scr[1, h, r % A_MID, pl.ds(q0 * A_MID + r // A_MID, n, stride=A_MID), :] = res

    def sweep(mode):
        for g, (_, dil) in enumerate(A_PATTERNS):
            q_ref, k_ref, v_ref = qkv[g]
            L = S // dil
            packed = L <= A_TQ
            rb = min(dil, 2 * A_TQ // L) if packed else 1
            tq = L if packed else A_TQ
            W = L if packed else min(tq + 2 * A_HALF, L)
            nq = L // tq
            ri = lax.broadcasted_iota(jnp.int32, (rb * tq, rb * W), 0)
            ci = lax.broadcasted_iota(jnp.int32, (rb * tq, rb * W), 1)
            band = (jnp.where((ri ^ ci) < L, jnp.abs(ci - ri), A_HALF + 1) <= A_HALF) if packed else None

            def tile(t, carry, g=g, dil=dil, L=L, tq=tq, W=W, nq=nq, rb=rb, packed=packed, band=band,
                     q_ref=q_ref, k_ref=k_ref, v_ref=v_ref, col_minus_row=ci - ri):
                if packed:
                    r0, q0, start, mask = t * rb, 0, 0, band
                    qt = q_ref[pl.ds(r0, rb), :, :].reshape(rb * L, LANES)
                    kw = k_ref[pl.ds(r0, rb), :, :].reshape(rb * L, LANES)
                else:
                    r0 = t // nq
                    q0 = pl.multiple_of((t % nq) * tq, tq)
                    start = pl.multiple_of(jnp.clip(q0 - A_HALF, 0, L - W), A_HALF)
                    mask = jnp.abs(col_minus_row + (start - q0)) <= A_HALF
                    qt = q_ref[r0, pl.ds(q0, tq), :]
                    kw = k_ref[r0, pl.ds(start, W), :]
                for h in range(2):
                    sl = slice(h * HEAD_DIM, (h + 1) * HEAD_DIM)
                    s = lax.dot_general(qt[:, sl], kw[:, sl], dn_nt, preferred_element_type=F32)
                    if mode == "max":
                        mrow = jnp.max(jnp.where(mask, s, -jnp.inf), axis=-1, keepdims=True)
                        mrow = jnp.broadcast_to(mrow, (rb * tq, LANES))
                        for j in range(rb):
                            tok = tok_rows(dil, r0 + j, q0, tq)
                            mj = mrow[j * tq:(j + 1) * tq]
                            m_scr[h, tok, :] = mj if g == 0 else jnp.maximum(m_scr[h, tok, :], mj)
                    else:
                        if mode == "bound":
                            shift = bound
                        else:
                            shift = jnp.concatenate([m_scr[h, tok_rows(dil, r0 + j, q0, tq), :][:, 0:1]
                                                     for j in range(rb)], axis=0)
                        p = jnp.where(mask, jnp.exp2(s - shift), 0.0).astype(BF16)
                        if packed:
                            vw = v_ref[h, pl.ds(r0, rb), :, :].reshape(rb * L, LANES)
                        else:
                            vw = v_ref[h, r0, pl.ds(start, W), :]
                        res = jnp.dot(p, vw, preferred_element_type=F32)
                        for j in range(rb):
                            put_result(g, h, r0 + j, q0, res[j * tq:(j + 1) * tq])
                return carry

            trips = dil * nq // rb
            lax.fori_loop(0, trips, tile, 0, unroll=min(A_UNROLL, trips))

    @pl.when(bound <= SAFE_SHIFT)
    def _():
        sweep("bound")

    @pl.when(bound > SAFE_SHIFT)
    def _():
        sweep("max")
        sweep("exact")

    fin = min(A_FIN, S)
    lane = lax.broadcasted_iota(jnp.int32, (fin, LANES), 1)

    def finish(c, carry):
        rows = pl.ds(pl.multiple_of(c * fin, fin), fin)
        yrows = pl.ds(pl.multiple_of(c * (fin // A_MID), fin // A_MID), fin // A_MID)
        for h in range(2):
            for r in range(A_MID):
                tmp_scr[h, pl.ds(r, fin // A_MID, stride=A_MID), :] = (accy_scr[0, h, r, yrows, :]
                                                                       + accy_scr[1, h, r, yrows, :])
        x0, x1 = acc_scr[0, rows, :] + tmp_scr[0], acc_scr[1, rows, :] + tmp_scr[1]
        o0 = x0 / pltpu.roll(x0, HEAD_DIM, 1)
        o1 = pltpu.roll(x1, HEAD_DIM, 1) / x1
        o_ref[rows, :] = jnp.where(lane < HEAD_DIM, o0, o1).astype(BF16)
        return carry

    lax.fori_loop(0, S // fin, finish, 0)


def mixer_a(shift, a_qkv, Bn, S):
    in_specs = [pl.BlockSpec(memory_space=pltpu.SMEM)]
    for _, dil in A_PATTERNS:
        L = S // dil
        in_specs += [pl.BlockSpec((None, None, dil, L, LANES), lambda b, hp: (b, hp, 0, 0, 0)),
                     pl.BlockSpec((None, None, dil, L, LANES), lambda b, hp: (b, hp, 0, 0, 0)),
                     pl.BlockSpec((None, 2, dil, L, LANES), lambda b, hp: (b, hp, 0, 0, 0))]
    o = pl.pallas_call(
        functools.partial(_mixa_kernel, S=S),
        grid=(Bn, A_PAIRS),
        in_specs=in_specs,
        out_specs=pl.BlockSpec((None, S, LANES), lambda b, hp: (b, 0, hp)),
        out_shape=jax.ShapeDtypeStruct((Bn, S, A_WIDTH), BF16),
        scratch_shapes=[pltpu.VMEM((2, S, LANES), F32), pltpu.VMEM((2, 2, A_MID, S // A_MID, LANES), F32),
                        pltpu.VMEM((2, S, LANES), F32), pltpu.VMEM((2, min(A_FIN, S), LANES), F32)],
        compiler_params=_cparams(("arbitrary", "arbitrary")),
        name="mixer_a",
    )(shift, *a_qkv)
    return o.reshape(Bn * S, A_WIDTH)


SAFE_SHIFT = 56.0


def _score_bound(gq, gk):
    b = math.log2(math.e) * math.sqrt(HEAD_DIM) * jnp.max(jnp.abs(gq)) * jnp.max(jnp.abs(gk)) * 1.02
    return b.reshape(1).astype(F32)


def _mixc_kernel(shift_ref, q_ref, k_ref, v_ref, lam_ref, laminit_ref, g_ref, o_ref,
                 vaug_scr, c_scr, p1_scr, p2_scr, *, tq, tk, S):
    nk = S // tk
    dn_nt = (((1,), (1,)), ((), ()))
    maps = (slice(0, C_QK_DIM), slice(C_QK_DIM, 2 * C_QK_DIM))

    @pl.when(pl.program_id(2) == 0)
    def _():
        vaug_scr[:, :C_V_DIM] = v_ref[...]
        vaug_scr[:, C_V_DIM:] = jnp.ones((S, C_V_DIM), BF16)

    q = q_ref[...]
    bound = shift_ref[0]
    c_scr[...] = jnp.full((tq, LANES), bound, F32)

    @pl.when(bound > SAFE_SHIFT)
    def _():
        for sl in maps:
            m = jnp.full((tq, 1), -jnp.inf, F32)
            for j in range(nk):
                s = lax.dot_general(q[:, sl], k_ref[j * tk:(j + 1) * tk, sl], dn_nt, preferred_element_type=F32)
                m = jnp.maximum(m, jnp.max(s, axis=-1, keepdims=True))
            c_scr[:, sl] = jnp.broadcast_to(m, (tq, C_QK_DIM))

    for sl, p_scr in zip(maps, (p1_scr, p2_scr)):
        shift = c_scr[:, sl.start:sl.start + 1]
        for j in range(nk):
            s = lax.dot_general(q[:, sl], k_ref[j * tk:(j + 1) * tk, sl], dn_nt, preferred_element_type=F32)
            p_scr[:, j * tk:(j + 1) * tk] = jnp.exp2(s - shift).astype(BF16)
    a1 = jnp.dot(p1_scr[...], vaug_scr[...], preferred_element_type=F32)
    a2 = jnp.dot(p2_scr[...], vaug_scr[...], preferred_element_type=F32)
    lm = lam_ref[...]
    lam = (jnp.exp(jnp.sum(lm[0:1] * lm[1:2], axis=-1, keepdims=True))
           - jnp.exp(jnp.sum(lm[2:3] * lm[3:4], axis=-1, keepdims=True)) + laminit_ref[...])
    o = a1[:, :C_V_DIM] / a1[:, C_V_DIM:] - lam * (a2[:, :C_V_DIM] / a2[:, C_V_DIM:])
    ms = jnp.mean(o * o, axis=-1, keepdims=True)
    o_ref[...] = ((o * lax.rsqrt(ms + NORM_EPS)) * g_ref[...]).astype(BF16)


def mixer_c(qc, kc, vc, shift, c_lambda, lam_init, g_eff, Bn, S, tq=512, tk=512):
    tq, tk = min(tq, S), min(tk, S)
    q3, k3, v3 = (t.reshape(Bn, S, C_QK_WIDTH) for t in (qc, kc, vc))
    const = lambda shp: pl.BlockSpec(shp, lambda b, h, i: (0, 0))
    o = pl.pallas_call(
        functools.partial(_mixc_kernel, tq=tq, tk=tk, S=S),
        grid=(Bn, C_HEADS, S // tq),
        in_specs=[pl.BlockSpec(memory_space=pltpu.SMEM),
                  pl.BlockSpec((None, tq, LANES), lambda b, h, i: (b, i, h)),
                  pl.BlockSpec((None, S, LANES), lambda b, h, i: (b, 0, h)),
                  pl.BlockSpec((None, S, LANES), lambda b, h, i: (b, 0, h)),
                  const((4, C_QK_DIM)), const((1, 1)), const((1, LANES))],
        out_specs=pl.BlockSpec((None, tq, LANES), lambda b, h, i: (b, i, h)),
        out_shape=jax.ShapeDtypeStruct((Bn, S, C_WIDTH), BF16),
        scratch_shapes=[pltpu.VMEM((S, 2 * C_V_DIM), BF16), pltpu.VMEM((tq, LANES), F32),
                        pltpu.VMEM((tq, S), BF16), pltpu.VMEM((tq, S), BF16)],
        compiler_params=_cparams(("arbitrary", "arbitrary", "arbitrary")),
        name="mixer_c",
    )(shift, q3, k3, v3, c_lambda, lam_init, g_eff)
    return o.reshape(Bn * S, C_WIDTH)


def _bprep_kernel(c_ref, cp_ref, cn_ref, cw_ref, cb_ref, wq_ref, wkt_ref, q_ref, kt_ref, *, ts, nt):
    i = pl.program_id(1)
    c = c_ref[...]
    row = lax.broadcasted_iota(jnp.int32, (ts, 1), 0)
    prev_row = cp_ref[7:8, :] * jnp.where(i > 0, 1.0, 0.0)
    next_row = cn_ref[0:1, :] * jnp.where(i < nt - 1, 1.0, 0.0)
    c_prev = jnp.where(row == 0, prev_row, pltpu.roll(c, 1, 0))
    c_next = jnp.where(row == ts - 1, next_row, pltpu.roll(c, ts - 1, 0))
    cw = cw_ref[...]
    conv = cb_ref[...] + c_prev * cw[0:1] + c * cw[1:2] + c_next * cw[2:3]
    u = (conv * jax.nn.sigmoid(conv)).astype(BF16)
    kscale = 1.0 / math.sqrt(B_HEAD_DIM)
    for h in range(B_HEADS):
        sl = slice(h * B_HEAD_DIM, (h + 1) * B_HEAD_DIM)
        q_ref[:, sl] = jnp.dot(u[:, sl], wq_ref[h], preferred_element_type=F32).astype(BF16)
        kt_ref[h] = lax.dot_general(wkt_ref[h], u[:, sl], (((1,), (1,)), ((), ())),
                                    preferred_element_type=F32) * kscale


def mixer_b_prep(proj3, conv_w, conv_b, wq, wk, Bn, S, ts=512):
    nt = S // ts
    r8 = ts // 8
    nb8 = S // 8
    return pl.pallas_call(
        functools.partial(_bprep_kernel, ts=ts, nt=nt),
        grid=(Bn, nt),
        in_specs=[pl.BlockSpec((None, ts, B_WIDTH), lambda b, i: (b, i, COL_BC // B_WIDTH)),
                  pl.BlockSpec((None, 8, B_WIDTH), lambda b, i: (b, jnp.maximum(i * r8 - 1, 0), COL_BC // B_WIDTH)),
                  pl.BlockSpec((None, 8, B_WIDTH),
                               lambda b, i: (b, jnp.minimum((i + 1) * r8, nb8 - 1), COL_BC // B_WIDTH)),
                  pl.BlockSpec((3, B_WIDTH), lambda b, i: (0, 0)),
                  pl.BlockSpec((1, B_WIDTH), lambda b, i: (0, 0)),
                  pl.BlockSpec((B_HEADS, B_HEAD_DIM, B_HEAD_DIM), lambda b, i: (0, 0, 0)),
                  pl.BlockSpec((B_HEADS, B_HEAD_DIM, B_HEAD_DIM), lambda b, i: (0, 0, 0))],
        out_specs=[pl.BlockSpec((None, ts, B_WIDTH), lambda b, i: (b, i, 0)),
                   pl.BlockSpec((None, B_HEADS, B_HEAD_DIM, ts), lambda b, i: (b, 0, 0, i))],
        out_shape=[jax.ShapeDtypeStruct((Bn, S, B_WIDTH), BF16),
                   jax.ShapeDtypeStruct((Bn, B_HEADS, B_HEAD_DIM, S), F32)],
        compiler_params=_cparams(("arbitrary", "arbitrary")),
        name="mixer_b_prep",
    )(proj3, proj3, proj3, conv_w, conv_b, wq, wk)


def _split3(x):
    h1 = x.astype(BF16)
    r1 = x - h1.astype(F32)
    h2 = r1.astype(BF16)
    h3 = (r1 - h2.astype(F32)).astype(BF16)
    return h1, h2, h3


def _mlstm_kernel(qf_ref, ktf_ref, vf_ref, gf_ref, qb_ref, ktb_ref, vb_ref, gb_ref, bias_ref,
                  hf_ref, hb_ref, c_scr, m_scr, *, ts):
    L = MLSTM_CHUNK
    D = B_HEAD_DIM
    nch = ts // L

    @pl.when(pl.program_id(1) == 0)
    def _():
        c_scr[...] = jnp.zeros_like(c_scr)
        m_scr[...] = jnp.zeros_like(m_scr)

    row = lax.broadcasted_iota(jnp.int32, (L, L), 0)
    col = lax.broadcasted_iota(jnp.int32, (L, L), 1)
    bias = bias_ref[...]
    ones = jnp.ones((L, D), BF16)

    dirs = ((qf_ref, ktf_ref, vf_ref, gf_ref, hf_ref), (qb_ref, ktb_ref, vb_ref, gb_ref, hb_ref))
    masks = (col <= row, col >= row)
    lasts = (L - 1, 0)

    pre = {}
    for d in range(2):
        g_ref = dirs[d][3]
        tri = jnp.where(masks[d], 1.0, 0.0).astype(BF16)
        for cidx in range(nch):
            g = g_ref[cidx * L:(cidx + 1) * L, :] + bias
            lf = -(jnp.maximum(-g, 0.0) + jnp.log1p(jnp.exp(-jnp.abs(g))))
            s1, s2, s3 = _split3(lf)
            bsum = (jnp.dot(tri, s1, preferred_element_type=F32) + jnp.dot(tri, s2, preferred_element_type=F32)
                    + jnp.dot(tri, s3, preferred_element_type=F32))
            bsum = pltpu.roll(bsum, LANES - B_HEADS, 1)
            a_rows = (g - bsum).T
            for h in range(B_HEADS):
                j = 2 * B_HEADS * d + h
                a_row = a_rows[j:j + 1, :]
                a_max = jnp.max(jnp.where(masks[d], a_row, -jnp.inf), axis=1, keepdims=True)
                pre[d, cidx, h] = (a_row, bsum[:, j:j + 1], a_max)

    state = {(d, h): (c_scr[d, h], m_scr[d, h]) for d in range(2) for h in range(B_HEADS)}
    for step in range(nch):
        for d in range(2):
            q_ref, kt_ref, v_ref, _, h_ref = dirs[d]
            cidx = nch - 1 - step if d == 1 else step
            rows = slice(cidx * L, (cidx + 1) * L)
            last = lasts[d]
            for h in range(B_HEADS):
                sl = slice(h * D, (h + 1) * D)
                a_row, b_col, a_max = pre[d, cidx, h]
                caug, m_old = state[d, h]
                big_m = jnp.maximum(m_old, a_max)
                m_last = big_m[last:last + 1, :]
                q, kt = q_ref[rows, sl], kt_ref[h, :, rows]
                vaug = jnp.concatenate([v_ref[rows, sl].astype(BF16), ones], axis=1)
                w_intra = jnp.where(masks[d], jnp.exp(a_row - big_m), 0.0)
                sc = jnp.dot(q, kt.astype(BF16), preferred_element_type=F32) * w_intra
                na = (jnp.exp(m_old - big_m) * jnp.dot(q, caug.astype(BF16), preferred_element_type=F32)
                      + jnp.dot(sc.astype(BF16), vaug, preferred_element_type=F32))
                h_ref[rows, sl] = na[:, :D] / jnp.maximum(jnp.abs(na[:, D:]), jnp.exp(-(b_col + big_m)))
                wkt = (kt * jnp.exp(a_row - m_last)).astype(BF16)
                state[d, h] = (jnp.exp(m_old - m_last) * caug + jnp.dot(wkt, vaug, preferred_element_type=F32),
                               b_col[last:last + 1, :] + m_last)
    for (d, h), (caug, m) in state.items():
        c_scr[d, h] = caug
        m_scr[d, h] = m


def mixer_b(qb, ktb, proj3, gate_bias, Bn, S, ts=512):
    ts = min(ts, S)
    nt = S // ts
    fwd = lambda blk: (lambda b, i: (b, i, blk))
    bwd = lambda blk: (lambda b, i: (b, nt - 1 - i, blk))

    def specs(idx):
        seq = idx(0)
        return [pl.BlockSpec((None, ts, B_WIDTH), seq),
                pl.BlockSpec((None, B_HEADS, B_HEAD_DIM, ts), lambda b, i: (b, 0, 0, seq(b, i)[1])),
                pl.BlockSpec((None, ts, B_WIDTH), idx(COL_BV // B_WIDTH)),
                pl.BlockSpec((None, ts, LANES), idx(COL_BG // LANES))]

    return pl.pallas_call(
        functools.partial(_mlstm_kernel, ts=ts),
        grid=(Bn, nt),
        in_specs=specs(fwd) + specs(bwd) + [pl.BlockSpec((1, LANES), lambda b, i: (0, 0))],
        out_specs=[pl.BlockSpec((None, ts, B_WIDTH), fwd(0)), pl.BlockSpec((None, ts, B_WIDTH), bwd(0))],
        out_shape=[jax.ShapeDtypeStruct((Bn, S, B_WIDTH), F32)] * 2,
        scratch_shapes=[pltpu.VMEM((2, B_HEADS, B_HEAD_DIM, 2 * B_HEAD_DIM), F32),
                        pltpu.VMEM((2, B_HEADS, 1, 1), F32)],
        compiler_params=_cparams(("arbitrary", "arbitrary")),
        name="mixer_b",
    )(qb, ktb, proj3, proj3, qb, ktb, proj3, proj3, gate_bias)


def _merge_kernel(x_ref, n1g_ref, wgate_ref, oa_ref, hf_ref, hb_ref, bo_ref, oc_ref, hng_ref,
                  wa_ref, wb_ref, wc_ref, wo_ref, y_ref):
    x = x_ref[...]
    ms = jnp.mean(x * x, axis=-1, keepdims=True)
    xn = ((x * lax.rsqrt(ms + NORM_EPS)) * n1g_ref[...]).astype(BF16)
    gates = [jax.nn.sigmoid(jnp.dot(xn, wgate_ref[:, j * D_MODEL:(j + 1) * D_MODEL], preferred_element_type=F32))
             for j in range(N_BRANCH)]
    y_a = jnp.dot(oa_ref[...], wa_ref[...], preferred_element_type=F32)
    hsum = hf_ref[...] + hb_ref[...]
    hng = hng_ref[...]
    bo = bo_ref[...]
    parts = []
    for h in range(B_HEADS):
        sl = slice(h * B_HEAD_DIM, (h + 1) * B_HEAD_DIM)
        hh = hsum[:, sl]
        ms = jnp.mean(hh * hh, axis=-1, keepdims=True)
        parts.append(jax.nn.sigmoid(bo[:, sl]) * ((hh * lax.rsqrt(ms + NORM_EPS)) * hng[:, sl]))
    o_b = jnp.concatenate(parts, axis=-1)
    y_b = jnp.dot(o_b.astype(BF16), wb_ref[...], preferred_element_type=F32)
    y_c = jnp.dot(oc_ref[...], wc_ref[...], preferred_element_type=F32)
    mixed = gates[0] * y_a + gates[1] * y_b + gates[2] * y_c
    y_ref[...] = x + jnp.dot(mixed.astype(BF16), wo_ref[...], preferred_element_type=F32)


def merge(x2d, n1g, wgate, proj, oa, hf, hb, oc, hng, wa, wb, wc, wo, tm=512):
    T = x2d.shape[0]
    row = lambda w, blk=0: pl.BlockSpec((tm, w), lambda i: (i, blk))
    return pl.pallas_call(
        _merge_kernel,
        grid=(T // tm,),
        in_specs=[row(D_MODEL), _resident(n1g), _resident(wgate), row(A_WIDTH),
                  row(B_WIDTH), row(B_WIDTH), row(B_WIDTH, COL_BO // B_WIDTH), row(C_WIDTH),
                  _resident(hng), _resident(wa), _resident(wb), _resident(wc), _resident(wo)],
        out_specs=row(D_MODEL),
        out_shape=jax.ShapeDtypeStruct((T, D_MODEL), F32),
        compiler_params=_cparams(("arbitrary",)),
        name="merge",
    )(x2d, n1g, wgate, oa, hf, hb, proj, oc, hng, wa, wb, wc, wo)


FFN_CHUNK = D_FF // 2


def _ffn_kernel(x_ref, g_ref, wg_ref, wu_ref, wd_ref, y_ref):
    x = x_ref[...]
    ms = jnp.mean(x * x, axis=-1, keepdims=True)
    xn = ((x * lax.rsqrt(ms + NORM_EPS)) * g_ref[...]).astype(BF16)
    acc = x
    for c in range(D_FF // FFN_CHUNK):
        sl = slice(c * FFN_CHUNK, (c + 1) * FFN_CHUNK)
        hg = jnp.dot(xn, wg_ref[:, sl], preferred_element_type=F32)
        hu = jnp.dot(xn, wu_ref[:, sl], preferred_element_type=F32)
        a = (hg * jax.nn.sigmoid(hg)) * hu
        acc = acc + jnp.dot(a.astype(BF16), wd_ref[sl, :], preferred_element_type=F32)
    y_ref[...] = acc


def ffn(x2d, g, wg, wu, wd, tm=512):
    T = x2d.shape[0]
    return pl.pallas_call(
        _ffn_kernel,
        grid=(T // tm,),
        in_specs=[pl.BlockSpec((tm, D_MODEL), lambda i: (i, 0)),
                  _resident(g), _resident(wg), _resident(wu), _resident(wd)],
        out_specs=pl.BlockSpec((tm, D_MODEL), lambda i: (i, 0)),
        out_shape=jax.ShapeDtypeStruct((T, D_MODEL), F32),
        compiler_params=_cparams(("arbitrary",)),
        name="ffn",
    )(x2d, g, wg, wu, wd)


def _rotary_tables(S):
    inv = ROPE_THETA ** (-jnp.arange(0, ROT_DIM, 2, dtype=F32) / ROT_DIM)
    ang = jnp.arange(S, dtype=F32)[:, None] * inv[None, :]
    cos, sin = jnp.cos(ang), jnp.sin(ang)
    half = ROT_DIM // 2
    rest = HEAD_DIM - ROT_DIM
    one, zero, z8 = jnp.ones((S, rest), F32), jnp.zeros((S, rest), F32), jnp.zeros((S, half), F32)
    c64 = jnp.concatenate([cos, cos, one], axis=1)
    lo64 = jnp.concatenate([-sin, z8, zero], axis=1)
    hi64 = jnp.concatenate([z8, sin, zero], axis=1)
    return tuple(jnp.tile(t, (1, LANES // HEAD_DIM)) for t in (c64, lo64, hi64))


def _pack_layer(norm1_g, w_in, a_qn_g, a_kn_g, b_conv_w, b_conv_b, b_wq, b_wk, b_gate_bias, b_hn_g,
                c_qn_g, c_kn_g, c_lambda, c_hn_g, w_branch, w_out, norm2_g, w_ffn_in, w_ffn_out, layer_idx):
    o_bc = 3 * A_QKV
    o_bg = o_bc + 3 * B_WIDTH
    o_cq = o_bg + B_GATES
    o_gate = o_cq + 3 * C_QK_WIDTH
    w_att = jnp.concatenate([w_in[:, :o_bc], w_in[:, o_cq:o_gate]], axis=1).astype(BF16)
    w_b = jnp.concatenate([w_in[:, o_bc:o_bg], w_in[:, o_bg:o_cq], jnp.zeros((D_MODEL, LANES - B_GATES), F32)],
                          axis=1).astype(BF16)
    lam_init = 0.8 - 0.6 * math.exp(-0.3 * layer_idx)
    tile2 = lambda g: jnp.tile(g.reshape(1, HEAD_DIM), (1, LANES // HEAD_DIM))
    return dict(
        norm1_g=norm1_g.reshape(1, D_MODEL), w_att=w_att, w_b=w_b, w_gate=w_in[:, o_gate:].astype(BF16),
        gains=(tile2(a_qn_g), tile2(a_kn_g), tile2(c_qn_g), tile2(c_kn_g)),
        conv_w=b_conv_w, conv_b=b_conv_b.reshape(1, B_WIDTH), wq=b_wq.astype(BF16),
        wkt=jnp.swapaxes(b_wk, 1, 2).astype(BF16),
        gate_bias=jnp.concatenate([b_gate_bias.reshape(1, B_GATES), jnp.zeros((1, LANES - B_GATES), F32)], axis=1),
        hn_g=b_hn_g.reshape(1, B_WIDTH),
        a_shift=_score_bound(a_qn_g, a_kn_g), c_shift=_score_bound(c_qn_g, c_kn_g),
        c_lambda=c_lambda, lam_init=jnp.full((1, 1), lam_init, F32),
        c_hn_g=(c_hn_g * (1.0 - lam_init)).reshape(1, C_V_DIM),
        wa=w_branch[:A_WIDTH].astype(BF16), wb=w_branch[A_WIDTH:A_WIDTH + B_WIDTH].astype(BF16),
        wc=w_branch[A_WIDTH + B_WIDTH:].astype(BF16), wo=w_out.astype(BF16),
        norm2_g=norm2_g.reshape(1, D_MODEL),
        wg=w_ffn_in[:, :D_FF].astype(BF16), wu=w_ffn_in[:, D_FF:].astype(BF16), wd=w_ffn_out.astype(BF16))


def _layer(x2d, p, tables, bd, Bn, S):
    proj = norm_matmul(x2d, p["norm1_g"], p["w_b"])
    *a_qkv, qc, kc, vc = attn_proj(x2d, p["norm1_g"], p["w_att"], tables, bd, p["gains"], Bn, S)
    oa = mixer_a(p["a_shift"], a_qkv, Bn, S)
    proj3 = proj.reshape(Bn, S, PROJ_COLS)
    qb, ktb = mixer_b_prep(proj3, p["conv_w"], p["conv_b"], p["wq"], p["wkt"], Bn, S)
    hf, hb = mixer_b(qb, ktb, proj3, p["gate_bias"], Bn, S)
    oc = mixer_c(qc, kc, vc, p["c_shift"], p["c_lambda"], p["lam_init"], p["c_hn_g"], Bn, S)
    x2d = merge(x2d, p["norm1_g"], p["w_gate"], proj, oa,
                hf.reshape(Bn * S, B_WIDTH), hb.reshape(Bn * S, B_WIDTH), oc,
                p["hn_g"], p["wa"], p["wb"], p["wc"], p["wo"])
    return ffn(x2d, p["norm2_g"], p["wg"], p["wu"], p["wd"])


def _trunk(x, layers, bd):
    Bn, S, _ = x.shape
    tables = _rotary_tables(S)
    x2d = x.reshape(Bn * S, D_MODEL)
    for p in layers:
        x2d = _layer(x2d, p, tables, bd, Bn, S)
    return x2d.reshape(Bn, S, D_MODEL)


def kernel(x_prompt, x_sample, norm1_g, w_in, a_qn_g, a_kn_g, b_conv_w, b_conv_b, b_wq, b_wk, b_gate_bias, b_hn_g,
           c_qn_g, c_kn_g, c_lambda, c_hn_g, w_branch, w_out, norm2_g, w_ffn_in, w_ffn_out):
    weights = (norm1_g, w_in, a_qn_g, a_kn_g, b_conv_w, b_conv_b, b_wq, b_wk, b_gate_bias, b_hn_g,
               c_qn_g, c_kn_g, c_lambda, c_hn_g, w_branch, w_out, norm2_g, w_ffn_in, w_ffn_out)
    depth = norm1_g.shape[0]
    layers = [_pack_layer(*[w[l] for w in weights], layer_idx=l) for l in range(depth)]
    lane = jnp.arange(2 * LANES) // HEAD_DIM
    bd = (lane[:, None] == lane[None, :]).astype(BF16)
    return (_trunk(x_prompt, layers, bd), _trunk(x_sample, layers, bd))
```

```python
import functools
import math

import jax
import jax.numpy as jnp
from jax import lax
from jax.experimental import pallas as pl
from jax.experimental.pallas import tpu as pltpu

F32 = jnp.float32
BF16 = jnp.bfloat16

D_MODEL = 1024
DEPTH = 4
NORM_EPS = 1e-6
HEAD_DIM = 64
ROT_DIM = HEAD_DIM // 4
ROPE_THETA = 500000.0

A_PATTERNS = ((128, 1), (512, 4), (2048, 16))
A_GROUPS = len(A_PATTERNS)
A_HEADS = 6
A_WIDTH = A_HEADS * HEAD_DIM
A_QKV = A_GROUPS * A_WIDTH

B_HEADS = 4
B_HEAD_DIM = 128
B_WIDTH = B_HEADS * B_HEAD_DIM
B_GATES = 4 * B_HEADS

C_HEADS = 4
C_QK_DIM = HEAD_DIM
C_V_DIM = 2 * C_QK_DIM
C_QK_WIDTH = C_HEADS * 2 * C_QK_DIM
C_WIDTH = C_HEADS * C_V_DIM

N_BRANCH = 3
D_FF = -(-8 * D_MODEL // (3 * 256)) * 256

LANES = 128
VMEM_LIMIT = 56 * 1024 * 1024

COL_BC, COL_BV, COL_BO = 0, 512, 1024
COL_BG = 1536
PROJ_COLS = 1664
PROJ_TN = PROJ_COLS

MLSTM_CHUNK = 128


def _cparams(sem):
    return pltpu.CompilerParams(dimension_semantics=sem, vmem_limit_bytes=VMEM_LIMIT)


def _norm_matmul_kernel(x_ref, g_ref, w_ref, o_ref):
    x = x_ref[...]
    ms = jnp.mean(x * x, axis=-1, keepdims=True)
    y = (x * lax.rsqrt(ms + NORM_EPS)) * g_ref[...]
    o_ref[...] = jnp.dot(y.astype(BF16), w_ref[...], preferred_element_type=F32)


def norm_matmul(x2d, g, w, tm=512, tn=PROJ_TN):
    T, K = x2d.shape
    N = w.shape[1]
    return pl.pallas_call(
        _norm_matmul_kernel,
        grid=(N // tn, T // tm),
        in_specs=[pl.BlockSpec((tm, K), lambda j, i: (i, 0)),
                  pl.BlockSpec((1, K), lambda j, i: (0, 0)),
                  pl.BlockSpec((K, tn), lambda j, i: (0, j))],
        out_specs=pl.BlockSpec((tm, tn), lambda j, i: (i, j)),
        out_shape=jax.ShapeDtypeStruct((T, N), F32),
        compiler_params=_cparams(("arbitrary", "arbitrary")),
        name="in_proj",
    )(x2d, g, w)


def _resident(a):
    return pl.BlockSpec(a.shape, lambda *_: (0,) * a.ndim, pipeline_mode=pl.Buffered(1))


A_PAIRS = A_HEADS // 2
QSCALE = math.log2(math.e) / math.sqrt(HEAD_DIM)
ATT_AQ, ATT_AK, ATT_AV = 0, A_QKV, 2 * A_QKV
ATT_CQ, ATT_CK, ATT_CV = 3 * A_QKV, 3 * A_QKV + C_QK_WIDTH, 3 * A_QKV + 2 * C_QK_WIDTH
ATT_COLS = 3 * A_QKV + 3 * C_QK_WIDTH
ATT_DOT = ATT_COLS // 3


def _headnorm_rope(x, ss, g, cos, s_lo, s_hi, scale):
    r = lax.rsqrt(ss * (1.0 / HEAD_DIM) + NORM_EPS)
    xg = x * g
    rot = (xg * cos + pltpu.roll(xg, LANES - ROT_DIM // 2, 1) * s_lo + pltpu.roll(xg, ROT_DIM // 2, 1) * s_hi)
    return rot * (r * scale)


def _attn_proj_kernel(x_ref, n1g_ref, w_ref, cos_ref, slo_ref, shi_ref, bd_ref,
                      gaq_ref, gak_ref, gcq_ref, gck_ref,
                      q0_ref, k0_ref, v0_ref, q1_ref, k1_ref, v1_ref, q2_ref, k2_ref, v2_ref,
                      ocq_ref, ock_ref, ocv_ref, proj_scr, relay_scr, *, tm):
    x = x_ref[...]
    ms = jnp.mean(x * x, axis=-1, keepdims=True)
    xn = ((x * lax.rsqrt(ms + NORM_EPS)) * n1g_ref[...]).astype(BF16)
    for j in range(ATT_COLS // ATT_DOT):
        cols = slice(j * ATT_DOT, (j + 1) * ATT_DOT)
        proj_scr[:, cols] = jnp.dot(xn, w_ref[:, cols], preferred_element_type=F32)

    cos, s_lo, s_hi, bd = cos_ref[...], slo_ref[...], shi_ref[...], bd_ref[...]
    lane = lax.broadcasted_iota(jnp.int32, (tm, LANES), 1)

    def put(dst, idx, dil, y):
        if dil == 1:
            dst[idx, 0] = y.astype(BF16)
            return
        relay_scr[...] = y
        for r in range(dil):
            dst[idx, r] = relay_scr[pl.ds(r, tm // dil, stride=dil), :].astype(BF16)

    def normed_pair(off, params):
        x2 = proj_scr[:, off:off + 2 * LANES]
        ss = jnp.dot((x2 * x2).astype(BF16), bd, preferred_element_type=F32)
        return [_headnorm_rope(x2[:, h * LANES:(h + 1) * LANES], ss[:, h * LANES:(h + 1) * LANES],
                               params[h][0], cos, s_lo, s_hi, params[h][1]) for h in range(2)]

    n_a = A_QKV // LANES
    gaq, gak = gaq_ref[...], gak_ref[...]
    for p in range(n_a):
        chunks = (2 * p, 2 * p + 1)
        params = [(gaq, QSCALE) if c < n_a else (gak, 1.0) for c in chunks]
        for c, y in zip(chunks, normed_pair(ATT_AQ + 2 * p * LANES, params)):
            dsts = (q0_ref, q1_ref, q2_ref) if c < n_a else (k0_ref, k1_ref, k2_ref)
            grp, pair = divmod(c % n_a, A_PAIRS)
            put(dsts[grp], pair, A_PATTERNS[grp][1], y)
    for c in range(n_a):
        grp, pair = divmod(c, A_PAIRS)
        y = proj_scr[:, ATT_AV + c * LANES:ATT_AV + (c + 1) * LANES]
        dst = (v0_ref, v1_ref, v2_ref)[grp]
        put(dst, 2 * pair, A_PATTERNS[grp][1], jnp.where(lane < HEAD_DIM, y, 1.0))
        put(dst, 2 * pair + 1, A_PATTERNS[grp][1], jnp.where(lane < HEAD_DIM, pltpu.roll(y, HEAD_DIM, 1), 1.0))
    for off, dst, g_ref, scale in ((ATT_CQ, ocq_ref, gcq_ref, QSCALE), (ATT_CK, ock_ref, gck_ref, 1.0)):
        g = g_ref[...]
        for p in range(C_QK_WIDTH // (2 * LANES)):
            for h, y in enumerate(normed_pair(off + 2 * p * LANES, [(g, scale)] * 2)):
                c = 2 * p + h
                dst[:, c * LANES:(c + 1) * LANES] = y.astype(BF16)
    ocv_ref[...] = proj_scr[:, ATT_CV:ATT_CV + C_WIDTH].astype(BF16)


def attn_proj(x2d, n1g, w_att, tables, bd, gains, Bn, S, tm=512):
    T = x2d.shape[0]
    tm = min(tm, S)
    nps = S // tm
    tab = pl.BlockSpec((tm, LANES), lambda i: (i % nps, 0))
    out_specs, out_shape = [], []
    for _, dil in A_PATTERNS:
        for n in (A_PAIRS, A_PAIRS, A_HEADS):
            out_specs.append(pl.BlockSpec((None, n, dil, tm // dil, LANES), lambda i: (i // nps, 0, 0, i % nps, 0)))
            out_shape.append(jax.ShapeDtypeStruct((Bn, n, dil, S // dil, LANES), BF16))
    for _ in range(3):
        out_specs.append(pl.BlockSpec((tm, C_QK_WIDTH), lambda i: (i, 0)))
        out_shape.append(jax.ShapeDtypeStruct((T, C_QK_WIDTH), BF16))
    return pl.pallas_call(
        functools.partial(_attn_proj_kernel, tm=tm),
        grid=(T // tm,),
        in_specs=[pl.BlockSpec((tm, D_MODEL), lambda i: (i, 0)), _resident(n1g), _resident(w_att),
                  tab, tab, tab, _resident(bd)] + [_resident(g) for g in gains],
        out_specs=out_specs,
        out_shape=out_shape,
        scratch_shapes=[pltpu.VMEM((tm, ATT_COLS), F32), pltpu.VMEM((tm, LANES), F32)],
        compiler_params=_cparams(("arbitrary",)),
        name="attn_proj",
    )(x2d, n1g, w_att, *tables, bd, *gains)


A_HALF = 64
assert all(w // (2 * d) == A_HALF for w, d in A_PATTERNS)
A_TQ = 128
A_UNROLL = 16
A_FIN = 512
A_MID = A_PATTERNS[1][1]
assert tuple(d for _, d in A_PATTERNS) == (1, A_MID, A_MID * A_MID)


def _mixa_kernel(shift_ref, q0_ref, k0_ref, v0_ref, q1_ref, k1_ref, v1_ref, q2_ref, k2_ref, v2_ref, o_ref,
                 acc_scr, accy_scr, m_scr, tmp_scr, *, S):
    qkv = ((q0_ref, k0_ref, v0_ref), (q1_ref, k1_ref, v1_ref), (q2_ref, k2_ref, v2_ref))
    bound = shift_ref[0]
    dn_nt = (((1,), (1,)), ((), ()))

    def tok_rows(dil, r, q0, n):
        return pl.ds(q0, n) if dil == 1 else pl.ds(q0 * dil + r, n, stride=dil)

    def put_result(g, h, r, q0, res):
        n = res.shape[0]
        if g == 0:
            acc_scr[h, pl.ds(q0, n), :] = res
        elif g == 1:
            accy_scr[0, h, r, pl.ds(q0, n), :] = res
        else:
            accy_scr[1, h, r % A_MID, pl.ds(q0 * A_MID + r // A_MID, n, stride=A_MID), :] = res

    def sweep(mode):
        for g, (_, dil) in enumerate(A_PATTERNS):
            q_ref, k_ref, v_ref = qkv[g]
            L = S // dil
            packed = L <= A_TQ
            rb = min(dil, 2 * A_TQ // L) if packed else 1
            tq = L if packed else A_TQ
            W = L if packed else min(tq + 2 * A_HALF, L)
            nq = L // tq
            ri = lax.broadcasted_iota(jnp.int32, (rb * tq, rb * W), 0)
            ci = lax.broadcasted_iota(jnp.int32, (rb * tq, rb * W), 1)
            band = (jnp.where((ri ^ ci) < L, jnp.abs(ci - ri), A_HALF + 1) <= A_HALF) if packed else None

            def tile(t, carry, g=g, dil=dil, L=L, tq=tq, W=W, nq=nq, rb=rb, packed=packed, band=band,
                     q_ref=q_ref, k_ref=k_ref, v_ref=v_ref, col_minus_row=ci - ri):
                if packed:
                    r0, q0, start, mask = t * rb, 0, 0, band
                    qt = q_ref[pl.ds(r0, rb), :, :].reshape(rb * L, LANES)
                    kw = k_ref[pl.ds(r0, rb), :, :].reshape(rb * L, LANES)
                else:
                    r0 = t // nq
                    q0 = pl.multiple_of((t % nq) * tq, tq)
                    start = pl.multiple_of(jnp.clip(q0 - A_HALF, 0, L - W), A_HALF)
                    mask = jnp.abs(col_minus_row + (start - q0)) <= A_HALF
                    qt = q_ref[r0, pl.ds(q0, tq), :]
                    kw = k_ref[r0, pl.ds(start, W), :]
                for h in range(2):
                    sl = slice(h * HEAD_DIM, (h + 1) * HEAD_DIM)
                    s = lax.dot_general(qt[:, sl], kw[:, sl], dn_nt, preferred_element_type=F32)
                    if mode == "max":
                        mrow = jnp.max(jnp.where(mask, s, -jnp.inf), axis=-1, keepdims=True)
                        mrow = jnp.broadcast_to(mrow, (rb * tq, LANES))
                        for j in range(rb):
                            tok = tok_rows(dil, r0 + j, q0, tq)
                            mj = mrow[j * tq:(j + 1) * tq]
                            m_scr[h, tok, :] = mj if g == 0 else jnp.maximum(m_scr[h, tok, :], mj)
                    else:
                        if mode == "bound":
                            shift = bound
                        else:
                            shift = jnp.concatenate([m_scr[h, tok_rows(dil, r0 + j, q0, tq), :][:, 0:1]
                                                     for j in range(rb)], axis=0)
                        p = jnp.where(mask, jnp.exp2(s - shift), 0.0).astype(BF16)
                        if packed:
                            vw = v_ref[h, pl.ds(r0, rb), :, :].reshape(rb * L, LANES)
                        else:
                            vw = v_ref[h, r0, pl.ds(start, W), :]
                        res = jnp.dot(p, vw, preferred_element_type=F32)
                        for j in range(rb):
                            put_result(g, h, r0 + j, q0, res[j * tq:(j + 1) * tq])
                return carry

            trips = dil * nq // rb
            lax.fori_loop(0, trips, tile, 0, unroll=min(A_UNROLL, trips))

    @pl.when(bound <= SAFE_SHIFT)
    def _():
        sweep("bound")

    @pl.when(bound > SAFE_SHIFT)
    def _():
        sweep("max")
        sweep("exact")

    fin = min(A_FIN, S)
    lane = lax.broadcasted_iota(jnp.int32, (fin, LANES), 1)

    def finish(c, carry):
        rows = pl.ds(pl.multiple_of(c * fin, fin), fin)
        yrows = pl.ds(pl.multiple_of(c * (fin // A_MID), fin // A_MID), fin // A_MID)
        for h in range(2):
            for r in range(A_MID):
                tmp_scr[h, pl.ds(r, fin // A_MID, stride=A_MID), :] = (accy_scr[0, h, r, yrows, :]
                                                                       + accy_scr[1, h, r, yrows, :])
        x0, x1 = acc_scr[0, rows, :] + tmp_scr[0], acc_scr[1, rows, :] + tmp_scr[1]
        o0 = x0 / pltpu.roll(x0, HEAD_DIM, 1)
        o1 = pltpu.roll(x1, HEAD_DIM, 1) / x1
        o_ref[rows, :] = jnp.where(lane < HEAD_DIM, o0, o1).astype(BF16)
        return carry

    lax.fori_loop(0, S // fin, finish, 0)


def mixer_a(shift, a_qkv, Bn, S):
    in_specs = [pl.BlockSpec(memory_space=pltpu.SMEM)]
    for _, dil in A_PATTERNS:
        L = S // dil
        in_specs += [pl.BlockSpec((None, None, dil, L, LANES), lambda b, hp: (b, hp, 0, 0, 0)),
                     pl.BlockSpec((None, None, dil, L, LANES), lambda b, hp: (b, hp, 0, 0, 0)),
                     pl.BlockSpec((None, 2, dil, L, LANES), lambda b, hp: (b, hp, 0, 0, 0))]
    o = pl.pallas_call(
        functools.partial(_mixa_kernel, S=S),
        grid=(Bn, A_PAIRS),
        in_specs=in_specs,
        out_specs=pl.BlockSpec((None, S, LANES), lambda b, hp: (b, 0, hp)),
        out_shape=jax.ShapeDtypeStruct((Bn, S, A_WIDTH), BF16),
        scratch_shapes=[pltpu.VMEM((2, S, LANES), F32), pltpu.VMEM((2, 2, A_MID, S // A_MID, LANES), F32),
                        pltpu.VMEM((2, S, LANES), F32), pltpu.VMEM((2, min(A_FIN, S), LANES), F32)],
        compiler_params=_cparams(("arbitrary", "arbitrary")),
        name="mixer_a",
    )(shift, *a_qkv)
    return o.reshape(Bn * S, A_WIDTH)


SAFE_SHIFT = 56.0


def _score_bound(gq, gk):
    b = math.log2(math.e) * math.sqrt(HEAD_DIM) * jnp.max(jnp.abs(gq)) * jnp.max(jnp.abs(gk)) * 1.02
    return b.reshape(1).astype(F32)


def _mixc_kernel(shift_ref, q_ref, k_ref, v_ref, lam_ref, laminit_ref, g_ref, o_ref,
                 vaug_scr, c_scr, p1_scr, p2_scr, *, tq, tk, S):
    nk = S // tk
    dn_nt = (((1,), (1,)), ((), ()))
    maps = (slice(0, C_QK_DIM), slice(C_QK_DIM, 2 * C_QK_DIM))

    @pl.when(pl.program_id(2) == 0)
    def _():
        vaug_scr[:, :C_V_DIM] = v_ref[...]
        vaug_scr[:, C_V_DIM:] = jnp.ones((S, C_V_DIM), BF16)

    q = q_ref[...]
    bound = shift_ref[0]
    c_scr[...] = jnp.full((tq, LANES), bound, F32)

    @pl.when(bound > SAFE_SHIFT)
    def _():
        for sl in maps:
            m = jnp.full((tq, 1), -jnp.inf, F32)
            for j in range(nk):
                s = lax.dot_general(q[:, sl], k_ref[j * tk:(j + 1) * tk, sl], dn_nt, preferred_element_type=F32)
                m = jnp.maximum(m, jnp.max(s, axis=-1, keepdims=True))
            c_scr[:, sl] = jnp.broadcast_to(m, (tq, C_QK_DIM))

    for sl, p_scr in zip(maps, (p1_scr, p2_scr)):
        shift = c_scr[:, sl.start:sl.start + 1]
        for j in range(nk):
            s = lax.dot_general(q[:, sl], k_ref[j * tk:(j + 1) * tk, sl], dn_nt, preferred_element_type=F32)
            p_scr[:, j * tk:(j + 1) * tk] = jnp.exp2(s - shift).astype(BF16)
    a1 = jnp.dot(p1_scr[...], vaug_scr[...], preferred_element_type=F32)
    a2 = jnp.dot(p2_scr[...], vaug_scr[...], preferred_element_type=F32)
    lm = lam_ref[...]
    lam = (jnp.exp(jnp.sum(lm[0:1] * lm[1:2], axis=-1, keepdims=True))
           - jnp.exp(jnp.sum(lm[2:3] * lm[3:4], axis=-1, keepdims=True)) + laminit_ref[...])
    o = a1[:, :C_V_DIM] / a1[:, C_V_DIM:] - lam * (a2[:, :C_V_DIM] / a2[:, C_V_DIM:])
    ms = jnp.mean(o * o, axis=-1, keepdims=True)
    o_ref[...] = ((o * lax.rsqrt(ms + NORM_EPS)) * g_ref[...]).astype(BF16)


def mixer_c(qc, kc, vc, shift, c_lambda, lam_init, g_eff, Bn, S, tq=512, tk=512):
    tq, tk = min(tq, S), min(tk, S)
    q3, k3, v3 = (t.reshape(Bn, S, C_QK_WIDTH) for t in (qc, kc, vc))
    const = lambda shp: pl.BlockSpec(shp, lambda b, h, i: (0, 0))
    o = pl.pallas_call(
        functools.partial(_mixc_kernel, tq=tq, tk=tk, S=S),
        grid=(Bn, C_HEADS, S // tq),
        in_specs=[pl.BlockSpec(memory_space=pltpu.SMEM),
                  pl.BlockSpec((None, tq, LANES), lambda b, h, i: (b, i, h)),
                  pl.BlockSpec((None, S, LANES), lambda b, h, i: (b, 0, h)),
                  pl.BlockSpec((None, S, LANES), lambda b, h, i: (b, 0, h)),
                  const((4, C_QK_DIM)), const((1, 1)), const((1, LANES))],
        out_specs=pl.BlockSpec((None, tq, LANES), lambda b, h, i: (b, i, h)),
        out_shape=jax.ShapeDtypeStruct((Bn, S, C_WIDTH), BF16),
        scratch_shapes=[pltpu.VMEM((S, 2 * C_V_DIM), BF16), pltpu.VMEM((tq, LANES), F32),
                        pltpu.VMEM((tq, S), BF16), pltpu.VMEM((tq, S), BF16)],
        compiler_params=_cparams(("arbitrary", "arbitrary", "arbitrary")),
        name="mixer_c",
    )(shift, q3, k3, v3, c_lambda, lam_init, g_eff)
    return o.reshape(Bn * S, C_WIDTH)


def _bprep_kernel(c_ref, cp_ref, cn_ref, cw_ref, cb_ref, wq_ref, wkt_ref, q_ref, kt_ref, *, ts, nt):
    i = pl.program_id(1)
    c = c_ref[...]
    row = lax.broadcasted_iota(jnp.int32, (ts, 1), 0)
    prev_row = cp_ref[7:8, :] * jnp.where(i > 0, 1.0, 0.0)
    next_row = cn_ref[0:1, :] * jnp.where(i < nt - 1, 1.0, 0.0)
    c_prev = jnp.where(row == 0, prev_row, pltpu.roll(c, 1, 0))
    c_next = jnp.where(row == ts - 1, next_row, pltpu.roll(c, ts - 1, 0))
    cw = cw_ref[...]
    conv = cb_ref[...] + c_prev * cw[0:1] + c * cw[1:2] + c_next * cw[2:3]
    u = (conv * jax.nn.sigmoid(conv)).astype(BF16)
    kscale = 1.0 / math.sqrt(B_HEAD_DIM)
    for h in range(B_HEADS):
        sl = slice(h * B_HEAD_DIM, (h + 1) * B_HEAD_DIM)
        q_ref[:, sl] = jnp.dot(u[:, sl], wq_ref[h], preferred_element_type=F32).astype(BF16)
        kt_ref[h] = lax.dot_general(wkt_ref[h], u[:, sl], (((1,), (1,)), ((), ())),
                                    preferred_element_type=F32) * kscale


def mixer_b_prep(proj3, conv_w, conv_b, wq, wk, Bn, S, ts=512):
    nt = S // ts
    r8 = ts // 8
    nb8 = S // 8
    return pl.pallas_call(
        functools.partial(_bprep_kernel, ts=ts, nt=nt),
        grid=(Bn, nt),
        in_specs=[pl.BlockSpec((None, ts, B_WIDTH), lambda b, i: (b, i, COL_BC // B_WIDTH)),
                  pl.BlockSpec((None, 8, B_WIDTH), lambda b, i: (b, jnp.maximum(i * r8 - 1, 0), COL_BC // B_WIDTH)),
                  pl.BlockSpec((None, 8, B_WIDTH),
                               lambda b, i: (b, jnp.minimum((i + 1) * r8, nb8 - 1), COL_BC // B_WIDTH)),
                  pl.BlockSpec((3, B_WIDTH), lambda b, i: (0, 0)),
                  pl.BlockSpec((1, B_WIDTH), lambda b, i: (0, 0)),
                  pl.BlockSpec((B_HEADS, B_HEAD_DIM, B_HEAD_DIM), lambda b, i: (0, 0, 0)),
                  pl.BlockSpec((B_HEADS, B_HEAD_DIM, B_HEAD_DIM), lambda b, i: (0, 0, 0))],
        out_specs=[pl.BlockSpec((None, ts, B_WIDTH), lambda b, i: (b, i, 0)),
                   pl.BlockSpec((None, B_HEADS, B_HEAD_DIM, ts), lambda b, i: (b, 0, 0, i))],
        out_shape=[jax.ShapeDtypeStruct((Bn, S, B_WIDTH), BF16),
                   jax.ShapeDtypeStruct((Bn, B_HEADS, B_HEAD_DIM, S), F32)],
        compiler_params=_cparams(("arbitrary", "arbitrary")),
        name="mixer_b_prep",
    )(proj3, proj3, proj3, conv_w, conv_b, wq, wk)


def _split3(x):
    h1 = x.astype(BF16)
    r1 = x - h1.astype(F32)
    h2 = r1.astype(BF16)
    h3 = (r1 - h2.astype(F32)).astype(BF16)
    return h1, h2, h3


def _mlstm_kernel(qf_ref, ktf_ref, vf_ref, gf_ref, qb_ref, ktb_ref, vb_ref, gb_ref, bias_ref,
                  hf_ref, hb_ref, c_scr, m_scr, *, ts):
    L = MLSTM_CHUNK
    D = B_HEAD_DIM
    nch = ts // L

    @pl.when(pl.program_id(1) == 0)
    def _():
        c_scr[...] = jnp.zeros_like(c_scr)
        m_scr[...] = jnp.zeros_like(m_scr)

    row = lax.broadcasted_iota(jnp.int32, (L, L), 0)
    col = lax.broadcasted_iota(jnp.int32, (L, L), 1)
    bias = bias_ref[...]
    ones = jnp.ones((L, D), BF16)

    dirs = ((qf_ref, ktf_ref, vf_ref, gf_ref, hf_ref), (qb_ref, ktb_ref, vb_ref, gb_ref, hb_ref))
    masks = (col <= row, col >= row)
    lasts = (L - 1, 0)

    pre = {}
    for d in range(2):
        g_ref = dirs[d][3]
        tri = jnp.where(masks[d], 1.0, 0.0).astype(BF16)
        for cidx in range(nch):
            g = g_ref[cidx * L:(cidx + 1) * L, :] + bias
            lf = -(jnp.maximum(-g, 0.0) + jnp.log1p(jnp.exp(-jnp.abs(g))))
            s1, s2, s3 = _split3(lf)
            bsum = (jnp.dot(tri, s1, preferred_element_type=F32) + jnp.dot(tri, s2, preferred_element_type=F32)
                    + jnp.dot(tri, s3, preferred_element_type=F32))
            bsum = pltpu.roll(bsum, LANES - B_HEADS, 1)
            a_rows = (g - bsum).T
            for h in range(B_HEADS):
                j = 2 * B_HEADS * d + h
                a_row = a_rows[j:j + 1, :]
                a_max = jnp.max(jnp.where(masks[d], a_row, -jnp.inf), axis=1, keepdims=True)
                pre[d, cidx, h] = (a_row, bsum[:, j:j + 1], a_max)

    state = {(d, h): (c_scr[d, h], m_scr[d, h]) for d in range(2) for h in range(B_HEADS)}
    combos = [(d, h) for d in range(2) for h in range(B_HEADS)]
    for step in range(nch):
        work = {}
        for d, h in combos:
            q_ref, kt_ref, v_ref, _, _ = dirs[d]
            cidx = nch - 1 - step if d == 1 else step
            rows = slice(cidx * L, (cidx + 1) * L)
            sl = slice(h * D, (h + 1) * D)
            a_row, b_col, a_max = pre[d, cidx, h]
            caug, m_old = state[d, h]
            big_m = jnp.maximum(m_old, a_max)
            q, kt = q_ref[rows, sl], kt_ref[h, :, rows]
            vaug = jnp.concatenate([v_ref[rows, sl].astype(BF16), ones], axis=1)
            qk = jnp.dot(q, kt.astype(BF16), preferred_element_type=F32)
            inter = jnp.dot(q, caug.astype(BF16), preferred_element_type=F32)
            work[d, h] = (rows, sl, a_row, b_col, big_m, kt, vaug, qk, inter)
        for d, h in combos:
            rows, sl, a_row, b_col, big_m, kt, vaug, qk, inter = work[d, h]
            m_old = state[d, h][1]
            sc = qk * jnp.where(masks[d], jnp.exp(a_row - big_m), 0.0)
            na = (jnp.exp(m_old - big_m) * inter
                  + jnp.dot(sc.astype(BF16), vaug, preferred_element_type=F32))
            work[d, h] = (rows, sl, a_row, b_col, big_m, kt, vaug, na)
        for d, h in combos:
            rows, sl, a_row, b_col, big_m, kt, vaug, na = work[d, h]
            caug, m_old = state[d, h]
            last = lasts[d]
            m_last = big_m[last:last + 1, :]
            dirs[d][4][rows, sl] = na[:, :D] / jnp.maximum(jnp.abs(na[:, D:]), jnp.exp(-(b_col + big_m)))
            wkt = (kt * jnp.exp(a_row - m_last)).astype(BF16)
            state[d, h] = (jnp.exp(m_old - m_last) * caug + jnp.dot(wkt, vaug, preferred_element_type=F32),
                           b_col[last:last + 1, :] + m_last)
    for (d, h), (caug, m) in state.items():
        c_scr[d, h] = caug
        m_scr[d, h] = m


def mixer_b(qb, ktb, proj3, gate_bias, Bn, S, ts=512):
    ts = min(ts, S)
    nt = S // ts
    fwd = lambda blk: (lambda b, i: (b, i, blk))
    bwd = lambda blk: (lambda b, i: (b, nt - 1 - i, blk))

    def specs(idx):
        seq = idx(0)
        return [pl.BlockSpec((None, ts, B_WIDTH), seq),
                pl.BlockSpec((None, B_HEADS, B_HEAD_DIM, ts), lambda b, i: (b, 0, 0, seq(b, i)[1])),
                pl.BlockSpec((None, ts, B_WIDTH), idx(COL_BV // B_WIDTH)),
                pl.BlockSpec((None, ts, LANES), idx(COL_BG // LANES))]

    return pl.pallas_call(
        functools.partial(_mlstm_kernel, ts=ts),
        grid=(Bn, nt),
        in_specs=specs(fwd) + specs(bwd) + [pl.BlockSpec((1, LANES), lambda b, i: (0, 0))],
        out_specs=[pl.BlockSpec((None, ts, B_WIDTH), fwd(0)), pl.BlockSpec((None, ts, B_WIDTH), bwd(0))],
        out_shape=[jax.ShapeDtypeStruct((Bn, S, B_WIDTH), F32)] * 2,
        scratch_shapes=[pltpu.VMEM((2, B_HEADS, B_HEAD_DIM, 2 * B_HEAD_DIM), F32),
                        pltpu.VMEM((2, B_HEADS, 1, 1), F32)],
        compiler_params=_cparams(("arbitrary", "arbitrary")),
        name="mixer_b",
    )(qb, ktb, proj3, proj3, qb, ktb, proj3, proj3, gate_bias)


def _merge_kernel(x_ref, n1g_ref, wgate_ref, oa_ref, hf_ref, hb_ref, bo_ref, oc_ref, hng_ref,
                  wa_ref, wb_ref, wc_ref, wo_ref, y_ref):
    x = x_ref[...]
    ms = jnp.mean(x * x, axis=-1, keepdims=True)
    xn = ((x * lax.rsqrt(ms + NORM_EPS)) * n1g_ref[...]).astype(BF16)
    gates = [jax.nn.sigmoid(jnp.dot(xn, wgate_ref[:, j * D_MODEL:(j + 1) * D_MODEL], preferred_element_type=F32))
             for j in range(N_BRANCH)]
    y_a = jnp.dot(oa_ref[...], wa_ref[...], preferred_element_type=F32)
    hsum = hf_ref[...] + hb_ref[...]
    hng = hng_ref[...]
    bo = bo_ref[...]
    parts = []
    for h in range(B_HEADS):
        sl = slice(h * B_HEAD_DIM, (h + 1) * B_HEAD_DIM)
        hh = hsum[:, sl]
        ms = jnp.mean(hh * hh, axis=-1, keepdims=True)
        parts.append(jax.nn.sigmoid(bo[:, sl]) * ((hh * lax.rsqrt(ms + NORM_EPS)) * hng[:, sl]))
    o_b = jnp.concatenate(parts, axis=-1)
    y_b = jnp.dot(o_b.astype(BF16), wb_ref[...], preferred_element_type=F32)
    y_c = jnp.dot(oc_ref[...], wc_ref[...], preferred_element_type=F32)
    mixed = gates[0] * y_a + gates[1] * y_b + gates[2] * y_c
    y_ref[...] = x + jnp.dot(mixed.astype(BF16), wo_ref[...], preferred_element_type=F32)


def merge(x2d, n1g, wgate, proj, oa, hf, hb, oc, hng, wa, wb, wc, wo, tm=512):
    T = x2d.shape[0]
    row = lambda w, blk=0: pl.BlockSpec((tm, w), lambda i: (i, blk))
    return pl.pallas_call(
        _merge_kernel,
        grid=(T // tm,),
        in_specs=[row(D_MODEL), _resident(n1g), _resident(wgate), row(A_WIDTH),
                  row(B_WIDTH), row(B_WIDTH), row(B_WIDTH, COL_BO // B_WIDTH), row(C_WIDTH),
                  _resident(hng), _resident(wa), _resident(wb), _resident(wc), _resident(wo)],
        out_specs=row(D_MODEL),
        out_shape=jax.ShapeDtypeStruct((T, D_MODEL), F32),
        compiler_params=_cparams(("arbitrary",)),
        name="merge",
    )(x2d, n1g, wgate, oa, hf, hb, proj, oc, hng, wa, wb, wc, wo)


FFN_CHUNK = D_FF // 2


def _ffn_kernel(x_ref, g_ref, wg_ref, wu_ref, wd_ref, y_ref):
    x = x_ref[...]
    ms = jnp.mean(x * x, axis=-1, keepdims=True)
    xn = ((x * lax.rsqrt(ms + NORM_EPS)) * g_ref[...]).astype(BF16)
    acc = x
    for c in range(D_FF // FFN_CHUNK):
        sl = slice(c * FFN_CHUNK, (c + 1) * FFN_CHUNK)
        hg = jnp.dot(xn, wg_ref[:, sl], preferred_element_type=F32)
        hu = jnp.dot(xn, wu_ref[:, sl], preferred_element_type=F32)
        a = (hg * jax.nn.sigmoid(hg)) * hu
        acc = acc + jnp.dot(a.astype(BF16), wd_ref[sl, :], preferred_element_type=F32)
    y_ref[...] = acc


def ffn(x2d, g, wg, wu, wd, tm=512):
    T = x2d.shape[0]
    return pl.pallas_call(
        _ffn_kernel,
        grid=(T // tm,),
        in_specs=[pl.BlockSpec((tm, D_MODEL), lambda i: (i, 0)),
                  _resident(g), _resident(wg), _resident(wu), _resident(wd)],
        out_specs=pl.BlockSpec((tm, D_MODEL), lambda i: (i, 0)),
        out_shape=jax.ShapeDtypeStruct((T, D_MODEL), F32),
        compiler_params=_cparams(("arbitrary",)),
        name="ffn",
    )(x2d, g, wg, wu, wd)


def _rotary_tables(S):
    inv = ROPE_THETA ** (-jnp.arange(0, ROT_DIM, 2, dtype=F32) / ROT_DIM)
    ang = jnp.arange(S, dtype=F32)[:, None] * inv[None, :]
    cos, sin = jnp.cos(ang), jnp.sin(ang)
    half = ROT_DIM // 2
    rest = HEAD_DIM - ROT_DIM
    one, zero, z8 = jnp.ones((S, rest), F32), jnp.zeros((S, rest), F32), jnp.zeros((S, half), F32)
    c64 = jnp.concatenate([cos, cos, one], axis=1)
    lo64 = jnp.concatenate([-sin, z8, zero], axis=1)
    hi64 = jnp.concatenate([z8, sin, zero], axis=1)
    return tuple(jnp.tile(t, (1, LANES // HEAD_DIM)) for t in (c64, lo64, hi64))


def _pack_layer(norm1_g, w_in, a_qn_g, a_kn_g, b_conv_w, b_conv_b, b_wq, b_wk, b_gate_bias, b_hn_g,
                c_qn_g, c_kn_g, c_lambda, c_hn_g, w_branch, w_out, norm2_g, w_ffn_in, w_ffn_out, layer_idx):
    o_bc = 3 * A_QKV
    o_bg = o_bc + 3 * B_WIDTH
    o_cq = o_bg + B_GATES
    o_gate = o_cq + 3 * C_QK_WIDTH
    w_att = jnp.concatenate([w_in[:, :o_bc], w_in[:, o_cq:o_gate]], axis=1).astype(BF16)
    w_b = jnp.concatenate([w_in[:, o_bc:o_bg], w_in[:, o_bg:o_cq], jnp.zeros((D_MODEL, LANES - B_GATES), F32)],
                          axis=1).astype(BF16)
    lam_init = 0.8 - 0.6 * math.exp(-0.3 * layer_idx)
    tile2 = lambda g: jnp.tile(g.reshape(1, HEAD_DIM), (1, LANES // HEAD_DIM))
    return dict(
        norm1_g=norm1_g.reshape(1, D_MODEL), w_att=w_att, w_b=w_b, w_gate=w_in[:, o_gate:].astype(BF16),
        gains=(tile2(a_qn_g), tile2(a_kn_g), tile2(c_qn_g), tile2(c_kn_g)),
        conv_w=b_conv_w, conv_b=b_conv_b.reshape(1, B_WIDTH), wq=b_wq.astype(BF16),
        wkt=jnp.swapaxes(b_wk, 1, 2).astype(BF16),
        gate_bias=jnp.concatenate([b_gate_bias.reshape(1, B_GATES), jnp.zeros((1, LANES - B_GATES), F32)], axis=1),
        hn_g=b_hn_g.reshape(1, B_WIDTH),
        a_shift=_score_bound(a_qn_g, a_kn_g), c_shift=_score_bound(c_qn_g, c_kn_g),
        c_lambda=c_lambda, lam_init=jnp.full((1, 1), lam_init, F32),
        c_hn_g=(c_hn_g * (1.0 - lam_init)).reshape(1, C_V_DIM),
        wa=w_branch[:A_WIDTH].astype(BF16), wb=w_branch[A_WIDTH:A_WIDTH + B_WIDTH].astype(BF16),
        wc=w_branch[A_WIDTH + B_WIDTH:].astype(BF16), wo=w_out.astype(BF16),
        norm2_g=norm2_g.reshape(1, D_MODEL),
        wg=w_ffn_in[:, :D_FF].astype(BF16), wu=w_ffn_in[:, D_FF:].astype(BF16), wd=w_ffn_out.astype(BF16))


def _layer(x2d, p, tables, bd, Bn, S):
    proj = norm_matmul(x2d, p["norm1_g"], p["w_b"])
    *a_qkv, qc, kc, vc = attn_proj(x2d, p["norm1_g"], p["w_att"], tables, bd, p["gains"], Bn, S)
    oa = mixer_a(p["a_shift"], a_qkv, Bn, S)
    proj3 = proj.reshape(Bn, S, PROJ_COLS)
    qb, ktb = mixer_b_prep(proj3, p["conv_w"], p["conv_b"], p["wq"], p["wkt"], Bn, S)
    hf, hb = mixer_b(qb, ktb, proj3, p["gate_bias"], Bn, S)
    oc = mixer_c(qc, kc, vc, p["c_shift"], p["c_lambda"], p["lam_init"], p["c_hn_g"], Bn, S)
    x2d = merge(x2d, p["norm1_g"], p["w_gate"], proj, oa,
                hf.reshape(Bn * S, B_WIDTH), hb.reshape(Bn * S, B_WIDTH), oc,
                p["hn_g"], p["wa"], p["wb"], p["wc"], p["wo"])
    return ffn(x2d, p["norm2_g"], p["wg"], p["wu"], p["wd"])


def _trunk(x, layers, bd):
    Bn, S, _ = x.shape
    tables = _rotary_tables(S)
    x2d = x.reshape(Bn * S, D_MODEL)
    for p in layers:
        x2d = _layer(x2d, p, tables, bd, Bn, S)
    return x2d.reshape(Bn, S, D_MODEL)


def kernel(x_prompt, x_sample, norm1_g, w_in, a_qn_g, a_kn_g, b_conv_w, b_conv_b, b_wq, b_wk, b_gate_bias, b_hn_g,
           c_qn_g, c_kn_g, c_lambda, c_hn_g, w_branch, w_out, norm2_g, w_ffn_in, w_ffn_out):
    weights = (norm1_g, w_in, a_qn_g, a_kn_g, b_conv_w, b_conv_b, b_wq, b_wk, b_gate_bias, b_hn_g,
               c_qn_g, c_kn_g, c_lambda, c_hn_g, w_branch, w_out, norm2_g, w_ffn_in, w_ffn_out)
    depth = norm1_g.shape[0]
    layers = [_pack_layer(*[w[l] for w in weights], layer_idx=l) for l in range(depth)]
    lane = jnp.arange(2 * LANES) // HEAD_DIM
    bd = (lane[:, None] == lane[None, :]).astype(BF16)
    return (_trunk(x_prompt, layers, bd), _trunk(x_sample, layers, bd))
```

```python
import functools
import math

import jax
import jax.numpy as jnp
from jax import lax
from jax.experimental import pallas as pl
from jax.experimental.pallas import tpu as pltpu

F32 = jnp.float32
BF16 = jnp.bfloat16

D_MODEL = 1024
DEPTH = 4
NORM_EPS = 1e-6
HEAD_DIM = 64
ROT_DIM = HEAD_DIM // 4
ROPE_THETA = 500000.0

A_PATTERNS = ((128, 1), (512, 4), (2048, 16))
A_GROUPS = len(A_PATTERNS)
A_HEADS = 6
A_WIDTH = A_HEADS * HEAD_DIM
A_QKV = A_GROUPS * A_WIDTH

B_HEADS = 4
B_HEAD_DIM = 128
B_WIDTH = B_HEADS * B_HEAD_DIM
B_GATES = 4 * B_HEADS

C_HEADS = 4
C_QK_DIM = HEAD_DIM
C_V_DIM = 2 * C_QK_DIM
C_QK_WIDTH = C_HEADS * 2 * C_QK_DIM
C_WIDTH = C_HEADS * C_V_DIM

N_BRANCH = 3
D_FF = -(-8 * D_MODEL // (3 * 256)) * 256

LANES = 128
VMEM_LIMIT = 56 * 1024 * 1024

COL_BC, COL_BV, COL_BO = 0, 512, 1024
COL_BG = 1536
PROJ_COLS = 1664
PROJ_TN = PROJ_COLS

MLSTM_CHUNK = 128


def _cparams(sem):
    return pltpu.CompilerParams(dimension_semantics=sem, vmem_limit_bytes=VMEM_LIMIT)


def _norm_matmul_kernel(x_ref, g_ref, w_ref, o_ref):
    x = x_ref[...]
    ms = jnp.mean(x * x, axis=-1, keepdims=True)
    y = (x * lax.rsqrt(ms + NORM_EPS)) * g_ref[...]
    o_ref[...] = jnp.dot(y.astype(BF16), w_ref[...], preferred_element_type=F32)


def norm_matmul(x2d, g, w, tm=512, tn=PROJ_TN):
    T, K = x2d.shape
    N = w.shape[1]
    return pl.pallas_call(
        _norm_matmul_kernel,
        grid=(N // tn, T // tm),
        in_specs=[pl.BlockSpec((tm, K), lambda j, i: (i, 0)),
                  pl.BlockSpec((1, K), lambda j, i: (0, 0)),
                  pl.BlockSpec((K, tn), lambda j, i: (0, j))],
        out_specs=pl.BlockSpec((tm, tn), lambda j, i: (i, j)),
        out_shape=jax.ShapeDtypeStruct((T, N), F32),
        compiler_params=_cparams(("arbitrary", "arbitrary")),
        name="in_proj",
    )(x2d, g, w)


def _resident(a):
    return pl.BlockSpec(a.shape, lambda *_: (0,) * a.ndim, pipeline_mode=pl.Buffered(1))


A_PAIRS = A_HEADS // 2
QSCALE = math.log2(math.e) / math.sqrt(HEAD_DIM)
ATT_AQ, ATT_AK, ATT_AV = 0, A_QKV, 2 * A_QKV
ATT_CQ, ATT_CK, ATT_CV = 3 * A_QKV, 3 * A_QKV + C_QK_WIDTH, 3 * A_QKV + 2 * C_QK_WIDTH
ATT_COLS = 3 * A_QKV + 3 * C_QK_WIDTH
ATT_DOT = ATT_COLS // 3


def _headnorm_rope(x, ss, g, cos, s_lo, s_hi, scale):
    r = lax.rsqrt(ss * (1.0 / HEAD_DIM) + NORM_EPS)
    xg = x * g
    rot = (xg * cos + pltpu.roll(xg, LANES - ROT_DIM // 2, 1) * s_lo + pltpu.roll(xg, ROT_DIM // 2, 1) * s_hi)
    return rot * (r * scale)


def _attn_proj_kernel(x_ref, n1g_ref, w_ref, cos_ref, slo_ref, shi_ref, bd_ref,
                      gaq_ref, gak_ref, gcq_ref, gck_ref,
                      q0_ref, k0_ref, v0_ref, q1_ref, k1_ref, v1_ref, q2_ref, k2_ref, v2_ref,
                      ocq_ref, ock_ref, ocv_ref, proj_scr, relay_scr, *, tm):
    x = x_ref[...]
    ms = jnp.mean(x * x, axis=-1, keepdims=True)
    xn = ((x * lax.rsqrt(ms + NORM_EPS)) * n1g_ref[...]).astype(BF16)
    for j in range(ATT_COLS // ATT_DOT):
        cols = slice(j * ATT_DOT, (j + 1) * ATT_DOT)
        proj_scr[:, cols] = jnp.dot(xn, w_ref[:, cols], preferred_element_type=F32)

    cos, s_lo, s_hi, bd = cos_ref[...], slo_ref[...], shi_ref[...], bd_ref[...]
    lane = lax.broadcasted_iota(jnp.int32, (tm, LANES), 1)

    def put(dst, idx, dil, y):
        if dil == 1:
            dst[idx, 0] = y.astype(BF16)
            return
        relay_scr[...] = y
        for r in range(dil):
            dst[idx, r] = relay_scr[pl.ds(r, tm // dil, stride=dil), :].astype(BF16)

    def normed_pair(off, params):
        x2 = proj_scr[:, off:off + 2 * LANES]
        ss = jnp.dot((x2 * x2).astype(BF16), bd, preferred_element_type=F32)
        return [_headnorm_rope(x2[:, h * LANES:(h + 1) * LANES], ss[:, h * LANES:(h + 1) * LANES],
                               params[h][0], cos, s_lo, s_hi, params[h][1]) for h in range(2)]

    n_a = A_QKV // LANES
    gaq, gak = gaq_ref[...], gak_ref[...]
    for p in range(n_a):
        chunks = (2 * p, 2 * p + 1)
        params = [(gaq, QSCALE) if c < n_a else (gak, 1.0) for c in chunks]
        for c, y in zip(chunks, normed_pair(ATT_AQ + 2 * p * LANES, params)):
            dsts = (q0_ref, q1_ref, q2_ref) if c < n_a else (k0_ref, k1_ref, k2_ref)
            grp, pair = divmod(c % n_a, A_PAIRS)
            put(dsts[grp], pair, A_PATTERNS[grp][1], y)
    for c in range(n_a):
        grp, pair = divmod(c, A_PAIRS)
        y = proj_scr[:, ATT_AV + c * LANES:ATT_AV + (c + 1) * LANES]
        dst = (v0_ref, v1_ref, v2_ref)[grp]
        put(dst, 2 * pair, A_PATTERNS[grp][1], jnp.where(lane < HEAD_DIM, y, 1.0))
        put(dst, 2 * pair + 1, A_PATTERNS[grp][1], jnp.where(lane < HEAD_DIM, pltpu.roll(y, HEAD_DIM, 1), 1.0))
    for off, dst, g_ref, scale in ((ATT_CQ, ocq_ref, gcq_ref, QSCALE), (ATT_CK, ock_ref, gck_ref, 1.0)):
        g = g_ref[...]
        for p in range(C_QK_WIDTH // (2 * LANES)):
            for h, y in enumerate(normed_pair(off + 2 * p * LANES, [(g, scale)] * 2)):
                c = 2 * p + h
                dst[:, c * LANES:(c + 1) * LANES] = y.astype(BF16)
    ocv_ref[...] = proj_scr[:, ATT_CV:ATT_CV + C_WIDTH].astype(BF16)


def attn_proj(x2d, n1g, w_att, tables, bd, gains, Bn, S, tm=512):
    T = x2d.shape[0]
    tm = min(tm, S)
    nps = S // tm
    tab = pl.BlockSpec((tm, LANES), lambda i: (i % nps, 0))
    out_specs, out_shape = [], []
    for _, dil in A_PATTERNS:
        for n in (A_PAIRS, A_PAIRS, A_HEADS):
            out_specs.append(pl.BlockSpec((None, n, dil, tm // dil, LANES), lambda i: (i // nps, 0, 0, i % nps, 0)))
            out_shape.append(jax.ShapeDtypeStruct((Bn, n, dil, S // dil, LANES), BF16))
    for _ in range(3):
        out_specs.append(pl.BlockSpec((tm, C_QK_WIDTH), lambda i: (i, 0)))
        out_shape.append(jax.ShapeDtypeStruct((T, C_QK_WIDTH), BF16))
    return pl.pallas_call(
        functools.partial(_attn_proj_kernel, tm=tm),
        grid=(T // tm,),
        in_specs=[pl.BlockSpec((tm, D_MODEL), lambda i: (i, 0)), _resident(n1g), _resident(w_att),
                  tab, tab, tab, _resident(bd)] + [_resident(g) for g in gains],
        out_specs=out_specs,
        out_shape=out_shape,
        scratch_shapes=[pltpu.VMEM((tm, ATT_COLS), F32), pltpu.VMEM((tm, LANES), F32)],
        compiler_params=_cparams(("arbitrary",)),
        name="attn_proj",
    )(x2d, n1g, w_att, *tables, bd, *gains)


A_HALF = 64
assert all(w // (2 * d) == A_HALF for w, d in A_PATTERNS)
A_TQ = 128
A_UNROLL = 16
A_FIN = 512
A_MID = A_PATTERNS[1][1]
assert tuple(d for _, d in A_PATTERNS) == (1, A_MID, A_MID * A_MID)


def _mixa_kernel(shift_ref, q0_ref, k0_ref, v0_ref, q1_ref, k1_ref, v1_ref, q2_ref, k2_ref, v2_ref, o_ref,
                 acc_scr, accy_scr, m_scr, tmp_scr, *, S):
    qkv = ((q0_ref, k0_ref, v0_ref), (q1_ref, k1_ref, v1_ref), (q2_ref, k2_ref, v2_ref))
    bound = shift_ref[0]
    dn_nt = (((1,), (1,)), ((), ()))

    def tok_rows(dil, r, q0, n):
        return pl.ds(q0, n) if dil == 1 else pl.ds(q0 * dil + r, n, stride=dil)

    def put_result(g, h, r, q0, res):
        n = res.shape[0]
        if g == 0:
            acc_scr[h, pl.ds(q0, n), :] = res
        elif g == 1:
            accy_scr[0, h, r, pl.ds(q0, n), :] = res
        else:
            accy_scr[1, h, r % A_MID, pl.ds(q0 * A_MID + r // A_MID, n, stride=A_MID), :] = res

    def sweep(mode):
        for g, (_, dil) in enumerate(A_PATTERNS):
            q_ref, k_ref, v_ref = qkv[g]
            L = S // dil
            packed = L <= A_TQ
            rb = min(dil, 2 * A_TQ // L) if packed else 1
            tq = L if packed else A_TQ
            W = L if packed else min(tq + 2 * A_HALF, L)
            nq = L // tq
            ri = lax.broadcasted_iota(jnp.int32, (rb * tq, rb * W), 0)
            ci = lax.broadcasted_iota(jnp.int32, (rb * tq, rb * W), 1)
            band = (jnp.where((ri ^ ci) < L, jnp.abs(ci - ri), A_HALF + 1) <= A_HALF) if packed else None

            def tile(t, carry, g=g, dil=dil, L=L, tq=tq, W=W, nq=nq, rb=rb, packed=packed, band=band,
                     q_ref=q_ref, k_ref=k_ref, v_ref=v_ref, col_minus_row=ci - ri):
                if packed:
                    r0, q0, start, mask = t * rb, 0, 0, band
                    qt = q_ref[pl.ds(r0, rb), :, :].reshape(rb * L, LANES)
                    kw = k_ref[pl.ds(r0, rb), :, :].reshape(rb * L, LANES)
                else:
                    r0 = t // nq
                    q0 = pl.multiple_of((t % nq) * tq, tq)
                    start = pl.multiple_of(jnp.clip(q0 - A_HALF, 0, L - W), A_HALF)
                    mask = jnp.abs(col_minus_row + (start - q0)) <= A_HALF
                    qt = q_ref[r0, pl.ds(q0, tq), :]
                    kw = k_ref[r0, pl.ds(start, W), :]
                for h in range(2):
                    sl = slice(h * HEAD_DIM, (h + 1) * HEAD_DIM)
                    s = lax.dot_general(qt[:, sl], kw[:, sl], dn_nt, preferred_element_type=F32)
                    if mode == "max":
                        mrow = jnp.max(jnp.where(mask, s, -jnp.inf), axis=-1, keepdims=True)
                        mrow = jnp.broadcast_to(mrow, (rb * tq, LANES))
                        for j in range(rb):
                            tok = tok_rows(dil, r0 + j, q0, tq)
                            mj = mrow[j * tq:(j + 1) * tq]
                            m_scr[h, tok, :] = mj if g == 0 else jnp.maximum(m_scr[h, tok, :], mj)
                    else:
                        if mode == "bound":
                            shift = bound
                        else:
                            shift = jnp.concatenate([m_scr[h, tok_rows(dil, r0 + j, q0, tq), :][:, 0:1]
                                                     for j in range(rb)], axis=0)
                        p = jnp.where(mask, jnp.exp2(s - shift), 0.0).astype(BF16)
                        if packed:
                            vw = v_ref[h, pl.ds(r0, rb), :, :].reshape(rb * L, LANES)
                        else:
                            vw = v_ref[h, r0, pl.ds(start, W), :]
                        res = jnp.dot(p, vw, preferred_element_type=F32)
                        for j in range(rb):
                            put_result(g, h, r0 + j, q0, res[j * tq:(j + 1) * tq])
                return carry

            trips = dil * nq // rb
            lax.fori_loop(0, trips, tile, 0, unroll=min(A_UNROLL, trips))

    @pl.when(bound <= SAFE_SHIFT)
    def _():
        sweep("bound")

    @pl.when(bound > SAFE_SHIFT)
    def _():
        sweep("max")
        sweep("exact")

    fin = min(A_FIN, S)
    lane = lax.broadcasted_iota(jnp.int32, (fin, LANES), 1)

    def finish(c, carry):
        rows = pl.ds(pl.multiple_of(c * fin, fin), fin)
        yrows = pl.ds(pl.multiple_of(c * (fin // A_MID), fin // A_MID), fin // A_MID)
        for h in range(2):
            for r in range(A_MID):
                tmp_scr[h, pl.ds(r, fin // A_MID, stride=A_MID), :] = (accy_scr[0, h, r, yrows, :]
                                                                       + accy_scr[1, h, r, yrows, :])
        x0, x1 = acc_scr[0, rows, :] + tmp_scr[0], acc_scr[1, rows, :] + tmp_scr[1]
        o0 = x0 / pltpu.roll(x0, HEAD_DIM, 1)
        o1 = pltpu.roll(x1, HEAD_DIM, 1) / x1
        o_ref[rows, :] = jnp.where(lane < HEAD_DIM, o0, o1).astype(BF16)
        return carry

    lax.fori_loop(0, S // fin, finish, 0)


def mixer_a(shift, a_qkv, Bn, S):
    in_specs = [pl.BlockSpec(memory_space=pltpu.SMEM)]
    for _, dil in A_PATTERNS:
        L = S // dil
        in_specs += [pl.BlockSpec((None, None, dil, L, LANES), lambda b, hp: (b, hp, 0, 0, 0)),
                     pl.BlockSpec((None, None, dil, L, LANES), lambda b, hp: (b, hp, 0, 0, 0)),
                     pl.BlockSpec((None, 2, dil, L, LANES), lambda b, hp: (b, hp, 0, 0, 0))]
    o = pl.pallas_call(
        functools.partial(_mixa_kernel, S=S),
        grid=(Bn, A_PAIRS),
        in_specs=in_specs,
        out_specs=pl.BlockSpec((None, S, LANES), lambda b, hp: (b, 0, hp)),
        out_shape=jax.ShapeDtypeStruct((Bn, S, A_WIDTH), BF16),
        scratch_shapes=[pltpu.VMEM((2, S, LANES), F32), pltpu.VMEM((2, 2, A_MID, S // A_MID, LANES), F32),
                        pltpu.VMEM((2, S, LANES), F32), pltpu.VMEM((2, min(A_FIN, S), LANES), F32)],
        compiler_params=_cparams(("arbitrary", "arbitrary")),
        name="mixer_a",
    )(shift, *a_qkv)
    return o.reshape(Bn * S, A_WIDTH)


SAFE_SHIFT = 56.0


def _score_bound(gq, gk):
    b = math.log2(math.e) * math.sqrt(HEAD_DIM) * jnp.max(jnp.abs(gq)) * jnp.max(jnp.abs(gk)) * 1.02
    return b.reshape(1).astype(F32)


def _mixc_kernel(shift_ref, q_ref, k_ref, v_ref, lam_ref, laminit_ref, g_ref, o_ref,
                 vaug_scr, c_scr, p1_scr, p2_scr, *, tq, tk, S):
    nk = S // tk
    dn_nt = (((1,), (1,)), ((), ()))
    maps = (slice(0, C_QK_DIM), slice(C_QK_DIM, 2 * C_QK_DIM))

    @pl.when(pl.program_id(2) == 0)
    def _():
        vaug_scr[:, :C_V_DIM] = v_ref[...]
        vaug_scr[:, C_V_DIM:] = jnp.ones((S, C_V_DIM), BF16)

    q = q_ref[...]
    bound = shift_ref[0]
    c_scr[...] = jnp.full((tq, LANES), bound, F32)

    @pl.when(bound > SAFE_SHIFT)
    def _():
        for sl in maps:
            m = jnp.full((tq, 1), -jnp.inf, F32)
            for j in range(nk):
                s = lax.dot_general(q[:, sl], k_ref[j * tk:(j + 1) * tk, sl], dn_nt, preferred_element_type=F32)
                m = jnp.maximum(m, jnp.max(s, axis=-1, keepdims=True))
            c_scr[:, sl] = jnp.broadcast_to(m, (tq, C_QK_DIM))

    for sl, p_scr in zip(maps, (p1_scr, p2_scr)):
        shift = c_scr[:, sl.start:sl.start + 1]
        for j in range(nk):
            s = lax.dot_general(q[:, sl], k_ref[j * tk:(j + 1) * tk, sl], dn_nt, preferred_element_type=F32)
            p_scr[:, j * tk:(j + 1) * tk] = jnp.exp2(s - shift).astype(BF16)
    a1 = jnp.dot(p1_scr[...], vaug_scr[...], preferred_element_type=F32)
    a2 = jnp.dot(p2_scr[...], vaug_scr[...], preferred_element_type=F32)
    lm = lam_ref[...]
    lam = (jnp.exp(jnp.sum(lm[0:1] * lm[1:2], axis=-1, keepdims=True))
           - jnp.exp(jnp.sum(lm[2:3] * lm[3:4], axis=-1, keepdims=True)) + laminit_ref[...])
    o = a1[:, :C_V_DIM] / a1[:, C_V_DIM:] - lam * (a2[:, :C_V_DIM] / a2[:, C_V_DIM:])
    ms = jnp.mean(o * o, axis=-1, keepdims=True)
    o_ref[...] = ((o * lax.rsqrt(ms + NORM_EPS)) * g_ref[...]).astype(BF16)


def mixer_c(qc, kc, vc, shift, c_lambda, lam_init, g_eff, Bn, S, tq=512, tk=512):
    tq, tk = min(tq, S), min(tk, S)
    q3, k3, v3 = (t.reshape(Bn, S, C_QK_WIDTH) for t in (qc, kc, vc))
    const = lambda shp: pl.BlockSpec(shp, lambda b, h, i: (0, 0))
    o = pl.pallas_call(
        functools.partial(_mixc_kernel, tq=tq, tk=tk, S=S),
        grid=(Bn, C_HEADS, S // tq),
        in_specs=[pl.BlockSpec(memory_space=pltpu.SMEM),
                  pl.BlockSpec((None, tq, LANES), lambda b, h, i: (b, i, h)),
                  pl.BlockSpec((None, S, LANES), lambda b, h, i: (b, 0, h)),
                  pl.BlockSpec((None, S, LANES), lambda b, h, i: (b, 0, h)),
                  const((4, C_QK_DIM)), const((1, 1)), const((1, LANES))],
        out_specs=pl.BlockSpec((None, tq, LANES), lambda b, h, i: (b, i, h)),
        out_shape=jax.ShapeDtypeStruct((Bn, S, C_WIDTH), BF16),
        scratch_shapes=[pltpu.VMEM((S, 2 * C_V_DIM), BF16), pltpu.VMEM((tq, LANES), F32),
                        pltpu.VMEM((tq, S), BF16), pltpu.VMEM((tq, S), BF16)],
        compiler_params=_cparams(("arbitrary", "arbitrary", "arbitrary")),
        name="mixer_c",
    )(shift, q3, k3, v3, c_lambda, lam_init, g_eff)
    return o.reshape(Bn * S, C_WIDTH)


def _bprep_kernel(c_ref, cp_ref, cn_ref, cw_ref, cb_ref, wq_ref, wkt_ref, q_ref, kt_ref, *, ts, nt):
    i = pl.program_id(1)
    c = c_ref[...]
    row = lax.broadcasted_iota(jnp.int32, (ts, 1), 0)
    prev_row = cp_ref[7:8, :] * jnp.where(i > 0, 1.0, 0.0)
    next_row = cn_ref[0:1, :] * jnp.where(i < nt - 1, 1.0, 0.0)
    c_prev = jnp.where(row == 0, prev_row, pltpu.roll(c, 1, 0))
    c_next = jnp.where(row == ts - 1, next_row, pltpu.roll(c, ts - 1, 0))
    cw = cw_ref[...]
    conv = cb_ref[...] + c_prev * cw[0:1] + c * cw[1:2] + c_next * cw[2:3]
    u = (conv * jax.nn.sigmoid(conv)).astype(BF16)
    kscale = 1.0 / math.sqrt(B_HEAD_DIM)
    for h in range(B_HEADS):
        sl = slice(h * B_HEAD_DIM, (h + 1) * B_HEAD_DIM)
        q_ref[:, sl] = jnp.dot(u[:, sl], wq_ref[h], preferred_element_type=F32).astype(BF16)
        kt_ref[h] = lax.dot_general(wkt_ref[h], u[:, sl], (((1,), (1,)), ((), ())),
                                    preferred_element_type=F32) * kscale


def mixer_b_prep(proj3, conv_w, conv_b, wq, wk, Bn, S, ts=512):
    nt = S // ts
    r8 = ts // 8
    nb8 = S // 8
    return pl.pallas_call(
        functools.partial(_bprep_kernel, ts=ts, nt=nt),
        grid=(Bn, nt),
        in_specs=[pl.BlockSpec((None, ts, B_WIDTH), lambda b, i: (b, i, COL_BC // B_WIDTH)),
                  pl.BlockSpec((None, 8, B_WIDTH), lambda b, i: (b, jnp.maximum(i * r8 - 1, 0), COL_BC // B_WIDTH)),
                  pl.BlockSpec((None, 8, B_WIDTH),
                               lambda b, i: (b, jnp.minimum((i + 1) * r8, nb8 - 1), COL_BC // B_WIDTH)),
                  pl.BlockSpec((3, B_WIDTH), lambda b, i: (0, 0)),
                  pl.BlockSpec((1, B_WIDTH), lambda b, i: (0, 0)),
                  pl.BlockSpec((B_HEADS, B_HEAD_DIM, B_HEAD_DIM), lambda b, i: (0, 0, 0)),
                  pl.BlockSpec((B_HEADS, B_HEAD_DIM, B_HEAD_DIM), lambda b, i: (0, 0, 0))],
        out_specs=[pl.BlockSpec((None, ts, B_WIDTH), lambda b, i: (b, i, 0)),
                   pl.BlockSpec((None, B_HEADS, B_HEAD_DIM, ts), lambda b, i: (b, 0, 0, i))],
        out_shape=[jax.ShapeDtypeStruct((Bn, S, B_WIDTH), BF16),
                   jax.ShapeDtypeStruct((Bn, B_HEADS, B_HEAD_DIM, S), F32)],
        compiler_params=_cparams(("arbitrary", "arbitrary")),
        name="mixer_b_prep",
    )(proj3, proj3, proj3, conv_w, conv_b, wq, wk)


def _split3(x):
    h1 = x.astype(BF16)
    r1 = x - h1.astype(F32)
    h2 = r1.astype(BF16)
    h3 = (r1 - h2.astype(F32)).astype(BF16)
    return h1, h2, h3


def _mlstm_kernel(qf_ref, ktf_ref, vf_ref, gf_ref, qb_ref, ktb_ref, vb_ref, gb_ref, bias_ref,
                  hf_ref, hb_ref, c_scr, m_scr, *, ts):
    L = MLSTM_CHUNK
    D = B_HEAD_DIM
    nch = ts // L

    @pl.when(pl.program_id(1) == 0)
    def _():
        c_scr[...] = jnp.zeros_like(c_scr)
        m_scr[...] = jnp.zeros_like(m_scr)

    row = lax.broadcasted_iota(jnp.int32, (L, L), 0)
    col = lax.broadcasted_iota(jnp.int32, (L, L), 1)
    bias = bias_ref[...]
    ones = jnp.ones((L, D), BF16)

    dirs = ((qf_ref, ktf_ref, vf_ref, gf_ref, hf_ref), (qb_ref, ktb_ref, vb_ref, gb_ref, hb_ref))
    masks = (col <= row, col >= row)
    lasts = (L - 1, 0)

    pre = {}
    items = [(d, cidx) for d in range(2) for cidx in range(nch)]
    tris = [jnp.where(masks[d], 1.0, 0.0).astype(BF16) for d in range(2)]
    gs, bsums, arows = {}, {}, {}
    for d, cidx in items:
        gs[d, cidx] = dirs[d][3][cidx * L:(cidx + 1) * L, :] + bias
    for d, cidx in items:
        g = gs[d, cidx]
        lf = -(jnp.maximum(-g, 0.0) + jnp.log1p(jnp.exp(-jnp.abs(g))))
        s1, s2, s3 = _split3(lf)
        tri = tris[d]
        bsums[d, cidx] = (jnp.dot(tri, s1, preferred_element_type=F32) + jnp.dot(tri, s2, preferred_element_type=F32)
                          + jnp.dot(tri, s3, preferred_element_type=F32))
    for d, cidx in items:
        bsums[d, cidx] = pltpu.roll(bsums[d, cidx], LANES - B_HEADS, 1)
    for d, cidx in items:
        arows[d, cidx] = (gs[d, cidx] - bsums[d, cidx]).T
    for d, cidx in items:
        for h in range(B_HEADS):
            j = 2 * B_HEADS * d + h
            a_row = arows[d, cidx][j:j + 1, :]
            a_max = jnp.max(jnp.where(masks[d], a_row, -jnp.inf), axis=1, keepdims=True)
            pre[d, cidx, h] = (a_row, bsums[d, cidx][:, j:j + 1], a_max)

    state = {(d, h): (c_scr[d, h], m_scr[d, h]) for d in range(2) for h in range(B_HEADS)}
    combos = [(d, h) for d in range(2) for h in range(B_HEADS)]
    for step in range(nch):
        work = {}
        for d, h in combos:
            q_ref, kt_ref, v_ref, _, _ = dirs[d]
            cidx = nch - 1 - step if d == 1 else step
            rows = slice(cidx * L, (cidx + 1) * L)
            sl = slice(h * D, (h + 1) * D)
            a_row, b_col, a_max = pre[d, cidx, h]
            caug, m_old = state[d, h]
            big_m = jnp.maximum(m_old, a_max)
            q, kt = q_ref[rows, sl], kt_ref[h, :, rows]
            vaug = jnp.concatenate([v_ref[rows, sl].astype(BF16), ones], axis=1)
            qk = jnp.dot(q, kt.astype(BF16), preferred_element_type=F32)
            inter = jnp.dot(q, caug.astype(BF16), preferred_element_type=F32)
            work[d, h] = (rows, sl, a_row, b_col, big_m, kt, vaug, qk, inter)
        for d, h in combos:
            rows, sl, a_row, b_col, big_m, kt, vaug, qk, inter = work[d, h]
            m_old = state[d, h][1]
            sc = qk * jnp.where(masks[d], jnp.exp(a_row - big_m), 0.0)
            na = (jnp.exp(m_old - big_m) * inter
                  + jnp.dot(sc.astype(BF16), vaug, preferred_element_type=F32))
            work[d, h] = (rows, sl, a_row, b_col, big_m, kt, vaug, na)
        for d, h in combos:
            rows, sl, a_row, b_col, big_m, kt, vaug, na = work[d, h]
            caug, m_old = state[d, h]
            last = lasts[d]
            m_last = big_m[last:last + 1, :]
            dirs[d][4][rows, sl] = na[:, :D] / jnp.maximum(jnp.abs(na[:, D:]), jnp.exp(-(b_col + big_m)))
            wkt = (kt * jnp.exp(a_row - m_last)).astype(BF16)
            state[d, h] = (jnp.exp(m_old - m_last) * caug + jnp.dot(wkt, vaug, preferred_element_type=F32),
                           b_col[last:last + 1, :] + m_last)
    for (d, h), (caug, m) in state.items():
        c_scr[d, h] = caug
        m_scr[d, h] = m


def mixer_b(qb, ktb, proj3, gate_bias, Bn, S, ts=512):
    ts = min(ts, S)
    nt = S // ts
    fwd = lambda blk: (lambda b, i: (b, i, blk))
    bwd = lambda blk: (lambda b, i: (b, nt - 1 - i, blk))

    def specs(idx):
        seq = idx(0)
        return [pl.BlockSpec((None, ts, B_WIDTH), seq),
                pl.BlockSpec((None, B_HEADS, B_HEAD_DIM, ts), lambda b, i: (b, 0, 0, seq(b, i)[1])),
                pl.BlockSpec((None, ts, B_WIDTH), idx(COL_BV // B_WIDTH)),
                pl.BlockSpec((None, ts, LANES), idx(COL_BG // LANES))]

    return pl.pallas_call(
        functools.partial(_mlstm_kernel, ts=ts),
        grid=(Bn, nt),
        in_specs=specs(fwd) + specs(bwd) + [pl.BlockSpec((1, LANES), lambda b, i: (0, 0))],
        out_specs=[pl.BlockSpec((None, ts, B_WIDTH), fwd(0)), pl.BlockSpec((None, ts, B_WIDTH), bwd(0))],
        out_shape=[jax.ShapeDtypeStruct((Bn, S, B_WIDTH), F32)] * 2,
        scratch_shapes=[pltpu.VMEM((2, B_HEADS, B_HEAD_DIM, 2 * B_HEAD_DIM), F32),
                        pltpu.VMEM((2, B_HEADS, 1, 1), F32)],
        compiler_params=_cparams(("arbitrary", "arbitrary")),
        name="mixer_b",
    )(qb, ktb, proj3, proj3, qb, ktb, proj3, proj3, gate_bias)


def _merge_kernel(x_ref, n1g_ref, wgate_ref, oa_ref, hf_ref, hb_ref, bo_ref, oc_ref, hng_ref,
                  wa_ref, wb_ref, wc_ref, wo_ref, y_ref):
    x = x_ref[...]
    ms = jnp.mean(x * x, axis=-1, keepdims=True)
    xn = ((x * lax.rsqrt(ms + NORM_EPS)) * n1g_ref[...]).astype(BF16)
    gates = [jax.nn.sigmoid(jnp.dot(xn, wgate_ref[:, j * D_MODEL:(j + 1) * D_MODEL], preferred_element_type=F32))
             for j in range(N_BRANCH)]
    y_a = jnp.dot(oa_ref[...], wa_ref[...], preferred_element_type=F32)
    hsum = hf_ref[...] + hb_ref[...]
    hng = hng_ref[...]
    bo = bo_ref[...]
    parts = []
    for h in range(B_HEADS):
        sl = slice(h * B_HEAD_DIM, (h + 1) * B_HEAD_DIM)
        hh = hsum[:, sl]
        ms = jnp.mean(hh * hh, axis=-1, keepdims=True)
        parts.append(jax.nn.sigmoid(bo[:, sl]) * ((hh * lax.rsqrt(ms + NORM_EPS)) * hng[:, sl]))
    o_b = jnp.concatenate(parts, axis=-1)
    y_b = jnp.dot(o_b.astype(BF16), wb_ref[...], preferred_element_type=F32)
    y_c = jnp.dot(oc_ref[...], wc_ref[...], preferred_element_type=F32)
    mixed = gates[0] * y_a + gates[1] * y_b + gates[2] * y_c
    y_ref[...] = x + jnp.dot(mixed.astype(BF16), wo_ref[...], preferred_element_type=F32)


def merge(x2d, n1g, wgate, proj, oa, hf, hb, oc, hng, wa, wb, wc, wo, tm=512):
    T = x2d.shape[0]
    row = lambda w, blk=0: pl.BlockSpec((tm, w), lambda i: (i, blk))
    return pl.pallas_call(
        _merge_kernel,
        grid=(T // tm,),
        in_specs=[row(D_MODEL), _resident(n1g), _resident(wgate), row(A_WIDTH),
                  row(B_WIDTH), row(B_WIDTH), row(B_WIDTH, COL_BO // B_WIDTH), row(C_WIDTH),
                  _resident(hng), _resident(wa), _resident(wb), _resident(wc), _resident(wo)],
        out_specs=row(D_MODEL),
        out_shape=jax.ShapeDtypeStruct((T, D_MODEL), F32),
        compiler_params=_cparams(("arbitrary",)),
        name="merge",
    )(x2d, n1g, wgate, oa, hf, hb, proj, oc, hng, wa, wb, wc, wo)


def _ffn_kernel(x_ref, g_ref, wg_ref, wu_ref, wd_ref, y_ref):
    x = x_ref[...]
    ms = jnp.mean(x * x, axis=-1, keepdims=True)
    xn = ((x * lax.rsqrt(ms + NORM_EPS)) * g_ref[...]).astype(BF16)
    hg = jnp.dot(xn, wg_ref[...], preferred_element_type=F32)
    hu = jnp.dot(xn, wu_ref[...], preferred_element_type=F32)
    a = ((hg * jax.nn.sigmoid(hg)) * hu).astype(BF16)
    y_ref[...] = x + jnp.dot(a, wd_ref[...], preferred_element_type=F32)


def ffn(x2d, g, wg, wu, wd, tm=512):
    T = x2d.shape[0]
    return pl.pallas_call(
        _ffn_kernel,
        grid=(T // tm,),
        in_specs=[pl.BlockSpec((tm, D_MODEL), lambda i: (i, 0)),
                  _resident(g), _resident(wg), _resident(wu), _resident(wd)],
        out_specs=pl.BlockSpec((tm, D_MODEL), lambda i: (i, 0)),
        out_shape=jax.ShapeDtypeStruct((T, D_MODEL), F32),
        compiler_params=_cparams(("arbitrary",)),
        name="ffn",
    )(x2d, g, wg, wu, wd)


def _rotary_tables(S):
    inv = ROPE_THETA ** (-jnp.arange(0, ROT_DIM, 2, dtype=F32) / ROT_DIM)
    ang = jnp.arange(S, dtype=F32)[:, None] * inv[None, :]
    cos, sin = jnp.cos(ang), jnp.sin(ang)
    half = ROT_DIM // 2
    rest = HEAD_DIM - ROT_DIM
    one, zero, z8 = jnp.ones((S, rest), F32), jnp.zeros((S, rest), F32), jnp.zeros((S, half), F32)
    c64 = jnp.concatenate([cos, cos, one], axis=1)
    lo64 = jnp.concatenate([-sin, z8, zero], axis=1)
    hi64 = jnp.concatenate([z8, sin, zero], axis=1)
    return tuple(jnp.tile(t, (1, LANES // HEAD_DIM)) for t in (c64, lo64, hi64))


def _pack_layer(norm1_g, w_in, a_qn_g, a_kn_g, b_conv_w, b_conv_b, b_wq, b_wk, b_gate_bias, b_hn_g,
                c_qn_g, c_kn_g, c_lambda, c_hn_g, w_branch, w_out, norm2_g, w_ffn_in, w_ffn_out, layer_idx):
    o_bc = 3 * A_QKV
    o_bg = o_bc + 3 * B_WIDTH
    o_cq = o_bg + B_GATES
    o_gate = o_cq + 3 * C_QK_WIDTH
    w_att = jnp.concatenate([w_in[:, :o_bc], w_in[:, o_cq:o_gate]], axis=1).astype(BF16)
    w_b = jnp.concatenate([w_in[:, o_bc:o_bg], w_in[:, o_bg:o_cq], jnp.zeros((D_MODEL, LANES - B_GATES), F32)],
                          axis=1).astype(BF16)
    lam_init = 0.8 - 0.6 * math.exp(-0.3 * layer_idx)
    tile2 = lambda g: jnp.tile(g.reshape(1, HEAD_DIM), (1, LANES // HEAD_DIM))
    return dict(
        norm1_g=norm1_g.reshape(1, D_MODEL), w_att=w_att, w_b=w_b, w_gate=w_in[:, o_gate:].astype(BF16),
        gains=(tile2(a_qn_g), tile2(a_kn_g), tile2(c_qn_g), tile2(c_kn_g)),
        conv_w=b_conv_w, conv_b=b_conv_b.reshape(1, B_WIDTH), wq=b_wq.astype(BF16),
        wkt=jnp.swapaxes(b_wk, 1, 2).astype(BF16),
        gate_bias=jnp.concatenate([b_gate_bias.reshape(1, B_GATES), jnp.zeros((1, LANES - B_GATES), F32)], axis=1),
        hn_g=b_hn_g.reshape(1, B_WIDTH),
        a_shift=_score_bound(a_qn_g, a_kn_g), c_shift=_score_bound(c_qn_g, c_kn_g),
        c_lambda=c_lambda, lam_init=jnp.full((1, 1), lam_init, F32),
        c_hn_g=(c_hn_g * (1.0 - lam_init)).reshape(1, C_V_DIM),
        wa=w_branch[:A_WIDTH].astype(BF16), wb=w_branch[A_WIDTH:A_WIDTH + B_WIDTH].astype(BF16),
        wc=w_branch[A_WIDTH + B_WIDTH:].astype(BF16), wo=w_out.astype(BF16),
        norm2_g=norm2_g.reshape(1, D_MODEL),
        wg=w_ffn_in[:, :D_FF].astype(BF16), wu=w_ffn_in[:, D_FF:].astype(BF16), wd=w_ffn_out.astype(BF16))


def _layer(x2d, p, tables, bd, Bn, S):
    proj = norm_matmul(x2d, p["norm1_g"], p["w_b"])
    *a_qkv, qc, kc, vc = attn_proj(x2d, p["norm1_g"], p["w_att"], tables, bd, p["gains"], Bn, S)
    oa = mixer_a(p["a_shift"], a_qkv, Bn, S)
    proj3 = proj.reshape(Bn, S, PROJ_COLS)
    qb, ktb = mixer_b_prep(proj3, p["conv_w"], p["conv_b"], p["wq"], p["wkt"], Bn, S)
    hf, hb = mixer_b(qb, ktb, proj3, p["gate_bias"], Bn, S)
    oc = mixer_c(qc, kc, vc, p["c_shift"], p["c_lambda"], p["lam_init"], p["c_hn_g"], Bn, S)
    x2d = merge(x2d, p["norm1_g"], p["w_gate"], proj, oa,
                hf.reshape(Bn * S, B_WIDTH), hb.reshape(Bn * S, B_WIDTH), oc,
                p["hn_g"], p["wa"], p["wb"], p["wc"], p["wo"])
    return ffn(x2d, p["norm2_g"], p["wg"], p["wu"], p["wd"])


def _trunk(x, layers, bd):
    Bn, S, _ = x.shape
    tables = _rotary_tables(S)
    x2d = x.reshape(Bn * S, D_MODEL)
    for p in layers:
        x2d = _layer(x2d, p, tables, bd, Bn, S)
    return x2d.reshape(Bn, S, D_MODEL)


def kernel(x_prompt, x_sample, norm1_g, w_in, a_qn_g, a_kn_g, b_conv_w, b_conv_b, b_wq, b_wk, b_gate_bias, b_hn_g,
           c_qn_g, c_kn_g, c_lambda, c_hn_g, w_branch, w_out, norm2_g, w_ffn_in, w_ffn_out):
    weights = (norm1_g, w_in, a_qn_g, a_kn_g, b_conv_w, b_conv_b, b_wq, b_wk, b_gate_bias, b_hn_g,
               c_qn_g, c_kn_g, c_lambda, c_hn_g, w_branch, w_out, norm2_g, w_ffn_in, w_ffn_out)
    depth = norm1_g.shape[0]
    layers = [_pack_layer(*[w[l] for w in weights], layer_idx=l) for l in range(depth)]
    lane = jnp.arange(2 * LANES) // HEAD_DIM
    bd = (lane[:, None] == lane[None, :]).astype(BF16)
    return (_trunk(x_prompt, layers, bd), _trunk(x_sample, layers, bd))
```

```python
import functools
import math

import jax
import jax.numpy as jnp
from jax import lax
from jax.experimental import pallas as pl
from jax.experimental.pallas import tpu as pltpu

F32 = jnp.float32
BF16 = jnp.bfloat16

D_MODEL = 1024
DEPTH = 4
NORM_EPS = 1e-6
HEAD_DIM = 64
ROT_DIM = HEAD_DIM // 4
ROPE_THETA = 500000.0

A_PATTERNS = ((128, 1), (512, 4), (2048, 16))
A_GROUPS = len(A_PATTERNS)
A_HEADS = 6
A_WIDTH = A_HEADS * HEAD_DIM
A_QKV = A_GROUPS * A_WIDTH

B_HEADS = 4
B_HEAD_DIM = 128
B_WIDTH = B_HEADS * B_HEAD_DIM
B_GATES = 4 * B_HEADS

C_HEADS = 4
C_QK_DIM = HEAD_DIM
C_V_DIM = 2 * C_QK_DIM
C_QK_WIDTH = C_HEADS * 2 * C_QK_DIM
C_WIDTH = C_HEADS * C_V_DIM

N_BRANCH = 3
D_FF = -(-8 * D_MODEL // (3 * 256)) * 256

LANES = 128
VMEM_LIMIT = 56 * 1024 * 1024

COL_BC, COL_BV, COL_BO = 0, 512, 1024
COL_BG = 1536
PROJ_COLS = 1664
PROJ_TN = PROJ_COLS

MLSTM_CHUNK = 128


def _cparams(sem):
    return pltpu.CompilerParams(dimension_semantics=sem, vmem_limit_bytes=VMEM_LIMIT)


def _norm_matmul_kernel(x_ref, g_ref, w_ref, o_ref):
    x = x_ref[...]
    ms = jnp.mean(x * x, axis=-1, keepdims=True)
    y = (x * lax.rsqrt(ms + NORM_EPS)) * g_ref[...]
    o_ref[...] = jnp.dot(y.astype(BF16), w_ref[...], preferred_element_type=F32)


def norm_matmul(x2d, g, w, tm=512, tn=PROJ_TN):
    T, K = x2d.shape
    N = w.shape[1]
    return pl.pallas_call(
        _norm_matmul_kernel,
        grid=(N // tn, T // tm),
        in_specs=[pl.BlockSpec((tm, K), lambda j, i: (i, 0)),
                  pl.BlockSpec((1, K), lambda j, i: (0, 0)),
                  pl.BlockSpec((K, tn), lambda j, i: (0, j))],
        out_specs=pl.BlockSpec((tm, tn), lambda j, i: (i, j)),
        out_shape=jax.ShapeDtypeStruct((T, N), F32),
        compiler_params=_cparams(("arbitrary", "arbitrary")),
        name="in_proj",
    )(x2d, g, w)


def _resident(a):
    return pl.BlockSpec(a.shape, lambda *_: (0,) * a.ndim, pipeline_mode=pl.Buffered(1))


A_PAIRS = A_HEADS // 2
QSCALE = math.log2(math.e) / math.sqrt(HEAD_DIM)
ATT_AQ, ATT_AK, ATT_AV = 0, A_QKV, 2 * A_QKV
ATT_CQ, ATT_CK, ATT_CV = 3 * A_QKV, 3 * A_QKV + C_QK_WIDTH, 3 * A_QKV + 2 * C_QK_WIDTH
ATT_COLS = 3 * A_QKV + 3 * C_QK_WIDTH
ATT_DOT = ATT_COLS // 13


def _headnorm_rope(x, ss, g, cos, s_lo, s_hi, scale):
    r = lax.rsqrt(ss * (1.0 / HEAD_DIM) + NORM_EPS)
    xg = x * g
    rot = (xg * cos + pltpu.roll(xg, LANES - ROT_DIM // 2, 1) * s_lo + pltpu.roll(xg, ROT_DIM // 2, 1) * s_hi)
    return rot * (r * scale)


def _attn_proj_kernel(x_ref, n1g_ref, w_ref, cos_ref, slo_ref, shi_ref, bd_ref,
                      gaq_ref, gak_ref, gcq_ref, gck_ref,
                      q0_ref, k0_ref, v0_ref, q1_ref, k1_ref, v1_ref, q2_ref, k2_ref, v2_ref,
                      ocq_ref, ock_ref, ocv_ref, proj_scr, relay_scr, *, tm):
    x = x_ref[...]
    ms = jnp.mean(x * x, axis=-1, keepdims=True)
    xn = ((x * lax.rsqrt(ms + NORM_EPS)) * n1g_ref[...]).astype(BF16)

    def project(j):
        cols = slice(j * ATT_DOT, (j + 1) * ATT_DOT)
        proj_scr[:, cols] = jnp.dot(xn, w_ref[:, cols], preferred_element_type=F32)

    cos, s_lo, s_hi, bd = cos_ref[...], slo_ref[...], shi_ref[...], bd_ref[...]
    lane = lax.broadcasted_iota(jnp.int32, (tm, LANES), 1)

    def put(dst, idx, dil, y):
        if dil == 1:
            dst[idx, 0] = y.astype(BF16)
            return
        relay_scr[...] = y
        for r in range(dil):
            dst[idx, r] = relay_scr[pl.ds(r, tm // dil, stride=dil), :].astype(BF16)

    def normed_pair(off, params):
        x2 = proj_scr[:, off:off + 2 * LANES]
        ss = jnp.dot((x2 * x2).astype(BF16), bd, preferred_element_type=F32)
        return [_headnorm_rope(x2[:, h * LANES:(h + 1) * LANES], ss[:, h * LANES:(h + 1) * LANES],
                               params[h][0], cos, s_lo, s_hi, params[h][1]) for h in range(2)]

    n_a = A_QKV // LANES
    gaq, gak = gaq_ref[...], gak_ref[...]

    def a_qk(p):
        chunks = (2 * p, 2 * p + 1)
        params = [(gaq, QSCALE) if c < n_a else (gak, 1.0) for c in chunks]
        for c, y in zip(chunks, normed_pair(ATT_AQ + 2 * p * LANES, params)):
            dsts = (q0_ref, q1_ref, q2_ref) if c < n_a else (k0_ref, k1_ref, k2_ref)
            grp, pair = divmod(c % n_a, A_PAIRS)
            put(dsts[grp], pair, A_PATTERNS[grp][1], y)

    def a_v(c):
        grp, pair = divmod(c, A_PAIRS)
        y = proj_scr[:, ATT_AV + c * LANES:ATT_AV + (c + 1) * LANES]
        dst = (v0_ref, v1_ref, v2_ref)[grp]
        put(dst, 2 * pair, A_PATTERNS[grp][1], jnp.where(lane < HEAD_DIM, y, 1.0))
        put(dst, 2 * pair + 1, A_PATTERNS[grp][1], jnp.where(lane < HEAD_DIM, pltpu.roll(y, HEAD_DIM, 1), 1.0))

    def c_qk(off, dst, g_ref, scale, p):
        for h, y in enumerate(normed_pair(off + 2 * p * LANES, [(g_ref[...], scale)] * 2)):
            c = 2 * p + h
            dst[:, c * LANES:(c + 1) * LANES] = y.astype(BF16)

    def c_v():
        ocv_ref[...] = proj_scr[:, ATT_CV:ATT_CV + C_WIDTH].astype(BF16)

    pieces = [(ATT_AQ + 2 * (p + 1) * LANES, functools.partial(a_qk, p)) for p in range(n_a)]
    pieces += [(ATT_AV + (c + 1) * LANES, functools.partial(a_v, c)) for c in range(n_a)]
    for off, dst, g_ref, scale in ((ATT_CQ, ocq_ref, gcq_ref, QSCALE), (ATT_CK, ock_ref, gck_ref, 1.0)):
        pieces += [(off + 2 * (p + 1) * LANES, functools.partial(c_qk, off, dst, g_ref, scale, p))
                   for p in range(C_QK_WIDTH // (2 * LANES))]
    pieces.append((ATT_COLS, c_v))
    for j in range(ATT_COLS // ATT_DOT):
        project(j)
        while pieces and pieces[0][0] <= (j + 1) * ATT_DOT:
            pieces.pop(0)[1]()
    assert not pieces


def attn_proj(x2d, n1g, w_att, tables, bd, gains, Bn, S, tm=512):
    T = x2d.shape[0]
    tm = min(tm, S)
    nps = S // tm
    tab = pl.BlockSpec((tm, LANES), lambda i: (i % nps, 0))
    out_specs, out_shape = [], []
    for _, dil in A_PATTERNS:
        for n in (A_PAIRS, A_PAIRS, A_HEADS):
            out_specs.append(pl.BlockSpec((None, n, dil, tm // dil, LANES), lambda i: (i // nps, 0, 0, i % nps, 0)))
            out_shape.append(jax.ShapeDtypeStruct((Bn, n, dil, S // dil, LANES), BF16))
    for _ in range(3):
        out_specs.append(pl.BlockSpec((tm, C_QK_WIDTH), lambda i: (i, 0)))
        out_shape.append(jax.ShapeDtypeStruct((T, C_QK_WIDTH), BF16))
    return pl.pallas_call(
        functools.partial(_attn_proj_kernel, tm=tm),
        grid=(T // tm,),
        in_specs=[pl.BlockSpec((tm, D_MODEL), lambda i: (i, 0)), _resident(n1g), _resident(w_att),
                  tab, tab, tab, _resident(bd)] + [_resident(g) for g in gains],
        out_specs=out_specs,
        out_shape=out_shape,
        scratch_shapes=[pltpu.VMEM((tm, ATT_COLS), F32), pltpu.VMEM((tm, LANES), F32)],
        compiler_params=_cparams(("arbitrary",)),
        name="attn_proj",
    )(x2d, n1g, w_att, *tables, bd, *gains)


A_HALF = 64
assert all(w // (2 * d) == A_HALF for w, d in A_PATTERNS)
A_TQ = 128
A_UNROLL = 16
A_FIN = 512
A_MID = A_PATTERNS[1][1]
assert tuple(d for _, d in A_PATTERNS) == (1, A_MID, A_MID * A_MID)


def _mixa_kernel(shift_ref, q0_ref, k0_ref, v0_ref, q1_ref, k1_ref, v1_ref, q2_ref, k2_ref, v2_ref, o_ref,
                 acc_scr, accy_scr, m_scr, tmp_scr, *, S):
    qkv = ((q0_ref, k0_ref, v0_ref), (q1_ref, k1_ref, v1_ref), (q2_ref, k2_ref, v2_ref))
    bound = shift_ref[0]
    dn_nt = (((1,), (1,)), ((), ()))

    def tok_rows(dil, r, q0, n):
        return pl.ds(q0, n) if dil == 1 else pl.ds(q0 * dil + r, n, stride=dil)

    def put_result(g, h, r, q0, res):
        n = res.shape[0]
        if g == 0:
            acc_scr[h, pl.ds(q0, n), :] = res
        elif g == 1:
            accy_scr[0, h, r, pl.ds(q0, n), :] = res
        else:
            accy_scr[1, h, r % A_MID, pl.ds(q0 * A_MID + r // A_MID, n, stride=A_MID), :] = res

    def sweep(mode):
        for g, (_, dil) in enumerate(A_PATTERNS):
            q_ref, k_ref, v_ref = qkv[g]
            L = S // dil
            packed = L <= A_TQ
            rb = min(dil, 2 * A_TQ // L) if packed else 1
            tq = L if packed else A_TQ
            W = L if packed else min(tq + 2 * A_HALF, L)
            nq = L // tq
            ri = lax.broadcasted_iota(jnp.int32, (rb * tq, rb * W), 0)
            ci = lax.broadcasted_iota(jnp.int32, (rb * tq, rb * W), 1)
            band = (jnp.where((ri ^ ci) < L, jnp.abs(ci - ri), A_HALF + 1) <= A_HALF) if packed else None

            def tile(t, carry, g=g, dil=dil, L=L, tq=tq, W=W, nq=nq, rb=rb, packed=packed, band=band,
                     q_ref=q_ref, k_ref=k_ref, v_ref=v_ref, col_minus_row=ci - ri):
                if packed:
                    r0, q0, start, mask = t * rb, 0, 0, band
                    qt = q_ref[pl.ds(r0, rb), :, :].reshape(rb * L, LANES)
                    kw = k_ref[pl.ds(r0, rb), :, :].reshape(rb * L, LANES)
                else:
                    r0 = t // nq
                    q0 = pl.multiple_of((t % nq) * tq, tq)
                    start = pl.multiple_of(jnp.clip(q0 - A_HALF, 0, L - W), A_HALF)
                    mask = jnp.abs(col_minus_row + (start - q0)) <= A_HALF
                    qt = q_ref[r0, pl.ds(q0, tq), :]
                    kw = k_ref[r0, pl.ds(start, W), :]
                for h in range(2):
                    sl = slice(h * HEAD_DIM, (h + 1) * HEAD_DIM)
                    s = lax.dot_general(qt[:, sl], kw[:, sl], dn_nt, preferred_element_type=F32)
                    if mode == "max":
                        mrow = jnp.max(jnp.where(mask, s, -jnp.inf), axis=-1, keepdims=True)
                        mrow = jnp.broadcast_to(mrow, (rb * tq, LANES))
                        for j in range(rb):
                            tok = tok_rows(dil, r0 + j, q0, tq)
                            mj = mrow[j * tq:(j + 1) * tq]
                            m_scr[h, tok, :] = mj if g == 0 else jnp.maximum(m_scr[h, tok, :], mj)
                    else:
                        if mode == "bound":
                            shift = bound
                        else:
                            shift = jnp.concatenate([m_scr[h, tok_rows(dil, r0 + j, q0, tq), :][:, 0:1]
                                                     for j in range(rb)], axis=0)
                        p = jnp.where(mask, jnp.exp2(s - shift), 0.0).astype(BF16)
                        if packed:
                            vw = v_ref[h, pl.ds(r0, rb), :, :].reshape(rb * L, LANES)
                        else:
                            vw = v_ref[h, r0, pl.ds(start, W), :]
                        res = jnp.dot(p, vw, preferred_element_type=F32)
                        for j in range(rb):
                            put_result(g, h, r0 + j, q0, res[j * tq:(j + 1) * tq])
                return carry

            trips = dil * nq // rb
            lax.fori_loop(0, trips, tile, 0, unroll=min(A_UNROLL, trips))

    @pl.when(bound <= SAFE_SHIFT)
    def _():
        sweep("bound")

    @pl.when(bound > SAFE_SHIFT)
    def _():
        sweep("max")
        sweep("exact")

    fin = min(A_FIN, S)
    lane = lax.broadcasted_iota(jnp.int32, (fin, LANES), 1)

    def finish(c, carry):
        rows = pl.ds(pl.multiple_of(c * fin, fin), fin)
        yrows = pl.ds(pl.multiple_of(c * (fin // A_MID), fin // A_MID), fin // A_MID)
        for h in range(2):
            for r in range(A_MID):
                tmp_scr[h, pl.ds(r, fin // A_MID, stride=A_MID), :] = (accy_scr[0, h, r, yrows, :]
                                                                       + accy_scr[1, h, r, yrows, :])
        x0, x1 = acc_scr[0, rows, :] + tmp_scr[0], acc_scr[1, rows, :] + tmp_scr[1]
        o0 = x0 / pltpu.roll(x0, HEAD_DIM, 1)
        o1 = pltpu.roll(x1, HEAD_DIM, 1) / x1
        o_ref[rows, :] = jnp.where(lane < HEAD_DIM, o0, o1).astype(BF16)
        return carry

    lax.fori_loop(0, S // fin, finish, 0)


def mixer_a(shift, a_qkv, Bn, S):
    in_specs = [pl.BlockSpec(memory_space=pltpu.SMEM)]
    for _, dil in A_PATTERNS:
        L = S // dil
        in_specs += [pl.BlockSpec((None, None, dil, L, LANES), lambda b, hp: (b, hp, 0, 0, 0)),
                     pl.BlockSpec((None, None, dil, L, LANES), lambda b, hp: (b, hp, 0, 0, 0)),
                     pl.BlockSpec((None, 2, dil, L, LANES), lambda b, hp: (b, hp, 0, 0, 0))]
    o = pl.pallas_call(
        functools.partial(_mixa_kernel, S=S),
        grid=(Bn, A_PAIRS),
        in_specs=in_specs,
        out_specs=pl.BlockSpec((None, S, LANES), lambda b, hp: (b, 0, hp)),
        out_shape=jax.ShapeDtypeStruct((Bn, S, A_WIDTH), BF16),
        scratch_shapes=[pltpu.VMEM((2, S, LANES), F32), pltpu.VMEM((2, 2, A_MID, S // A_MID, LANES), F32),
                        pltpu.VMEM((2, S, LANES), F32), pltpu.VMEM((2, min(A_FIN, S), LANES), F32)],
        compiler_params=_cparams(("arbitrary", "arbitrary")),
        name="mixer_a",
    )(shift, *a_qkv)
    return o.reshape(Bn * S, A_WIDTH)


SAFE_SHIFT = 56.0


def _score_bound(gq, gk):
    b = math.log2(math.e) * math.sqrt(HEAD_DIM) * jnp.max(jnp.abs(gq)) * jnp.max(jnp.abs(gk)) * 1.02
    return b.reshape(1).astype(F32)


def _mixc_kernel(shift_ref, q_ref, k_ref, v_ref, lam_ref, laminit_ref, g_ref, o_ref,
                 vaug_scr, c_scr, p1_scr, p2_scr, *, tq, tk, S):
    nk = S // tk
    dn_nt = (((1,), (1,)), ((), ()))
    maps = (slice(0, C_QK_DIM), slice(C_QK_DIM, 2 * C_QK_DIM))

    @pl.when(pl.program_id(2) == 0)
    def _():
        vaug_scr[:, :C_V_DIM] = v_ref[...]
        vaug_scr[:, C_V_DIM:] = jnp.ones((S, C_V_DIM), BF16)

    q = q_ref[...]
    bound = shift_ref[0]
    c_scr[...] = jnp.full((tq, LANES), bound, F32)

    @pl.when(bound > SAFE_SHIFT)
    def _():
        for sl in maps:
            m = jnp.full((tq, 1), -jnp.inf, F32)
            for j in range(nk):
                s = lax.dot_general(q[:, sl], k_ref[j * tk:(j + 1) * tk, sl], dn_nt, preferred_element_type=F32)
                m = jnp.maximum(m, jnp.max(s, axis=-1, keepdims=True))
            c_scr[:, sl] = jnp.broadcast_to(m, (tq, C_QK_DIM))

    for sl, p_scr in zip(maps, (p1_scr, p2_scr)):
        shift = c_scr[:, sl.start:sl.start + 1]
        for j in range(nk):
            s = lax.dot_general(q[:, sl], k_ref[j * tk:(j + 1) * tk, sl], dn_nt, preferred_element_type=F32)
            p_scr[:, j * tk:(j + 1) * tk] = jnp.exp2(s - shift).astype(BF16)
    a1 = jnp.dot(p1_scr[...], vaug_scr[...], preferred_element_type=F32)
    a2 = jnp.dot(p2_scr[...], vaug_scr[...], preferred_element_type=F32)
    lm = lam_ref[...]
    lam = (jnp.exp(jnp.sum(lm[0:1] * lm[1:2], axis=-1, keepdims=True))
           - jnp.exp(jnp.sum(lm[2:3] * lm[3:4], axis=-1, keepdims=True)) + laminit_ref[...])
    o = a1[:, :C_V_DIM] / a1[:, C_V_DIM:] - lam * (a2[:, :C_V_DIM] / a2[:, C_V_DIM:])
    ms = jnp.mean(o * o, axis=-1, keepdims=True)
    o_ref[...] = ((o * lax.rsqrt(ms + NORM_EPS)) * g_ref[...]).astype(BF16)


def mixer_c(qc, kc, vc, shift, c_lambda, lam_init, g_eff, Bn, S, tq=512, tk=512):
    tq, tk = min(tq, S), min(tk, S)
    q3, k3, v3 = (t.reshape(Bn, S, C_QK_WIDTH) for t in (qc, kc, vc))
    const = lambda shp: pl.BlockSpec(shp, lambda b, h, i: (0, 0))
    o = pl.pallas_call(
        functools.partial(_mixc_kernel, tq=tq, tk=tk, S=S),
        grid=(Bn, C_HEADS, S // tq),
        in_specs=[pl.BlockSpec(memory_space=pltpu.SMEM),
                  pl.BlockSpec((None, tq, LANES), lambda b, h, i: (b, i, h)),
                  pl.BlockSpec((None, S, LANES), lambda b, h, i: (b, 0, h)),
                  pl.BlockSpec((None, S, LANES), lambda b, h, i: (b, 0, h)),
                  const((4, C_QK_DIM)), const((1, 1)), const((1, LANES))],
        out_specs=pl.BlockSpec((None, tq, LANES), lambda b, h, i: (b, i, h)),
        out_shape=jax.ShapeDtypeStruct((Bn, S, C_WIDTH), BF16),
        scratch_shapes=[pltpu.VMEM((S, 2 * C_V_DIM), BF16), pltpu.VMEM((tq, LANES), F32),
                        pltpu.VMEM((tq, S), BF16), pltpu.VMEM((tq, S), BF16)],
        compiler_params=_cparams(("arbitrary", "arbitrary", "arbitrary")),
        name="mixer_c",
    )(shift, q3, k3, v3, c_lambda, lam_init, g_eff)
    return o.reshape(Bn * S, C_WIDTH)


def _bprep_kernel(c_ref, cp_ref, cn_ref, cw_ref, cb_ref, wq_ref, wkt_ref, q_ref, kt_ref, *, ts, nt):
    i = pl.program_id(1)
    c = c_ref[...]
    row = lax.broadcasted_iota(jnp.int32, (ts, 1), 0)
    prev_row = cp_ref[7:8, :] * jnp.where(i > 0, 1.0, 0.0)
    next_row = cn_ref[0:1, :] * jnp.where(i < nt - 1, 1.0, 0.0)
    c_prev = jnp.where(row == 0, prev_row, pltpu.roll(c, 1, 0))
    c_next = jnp.where(row == ts - 1, next_row, pltpu.roll(c, ts - 1, 0))
    cw = cw_ref[...]
    conv = cb_ref[...] + c_prev * cw[0:1] + c * cw[1:2] + c_next * cw[2:3]
    u = (conv * jax.nn.sigmoid(conv)).astype(BF16)
    kscale = 1.0 / math.sqrt(B_HEAD_DIM)
    for h in range(B_HEADS):
        sl = slice(h * B_HEAD_DIM, (h + 1) * B_HEAD_DIM)
        q_ref[:, sl] = jnp.dot(u[:, sl], wq_ref[h], preferred_element_type=F32).astype(BF16)
        kt_ref[h] = lax.dot_general(wkt_ref[h], u[:, sl], (((1,), (1,)), ((), ())),
                                    preferred_element_type=F32) * kscale


def mixer_b_prep(proj3, conv_w, conv_b, wq, wk, Bn, S, ts=512):
    nt = S // ts
    r8 = ts // 8
    nb8 = S // 8
    return pl.pallas_call(
        functools.partial(_bprep_kernel, ts=ts, nt=nt),
        grid=(Bn, nt),
        in_specs=[pl.BlockSpec((None, ts, B_WIDTH), lambda b, i: (b, i, COL_BC // B_WIDTH)),
                  pl.BlockSpec((None, 8, B_WIDTH), lambda b, i: (b, jnp.maximum(i * r8 - 1, 0), COL_BC // B_WIDTH)),
                  pl.BlockSpec((None, 8, B_WIDTH),
                               lambda b, i: (b, jnp.minimum((i + 1) * r8, nb8 - 1), COL_BC // B_WIDTH)),
                  pl.BlockSpec((3, B_WIDTH), lambda b, i: (0, 0)),
                  pl.BlockSpec((1, B_WIDTH), lambda b, i: (0, 0)),
                  pl.BlockSpec((B_HEADS, B_HEAD_DIM, B_HEAD_DIM), lambda b, i: (0, 0, 0)),
                  pl.BlockSpec((B_HEADS, B_HEAD_DIM, B_HEAD_DIM), lambda b, i: (0, 0, 0))],
        out_specs=[pl.BlockSpec((None, ts, B_WIDTH), lambda b, i: (b, i, 0)),
                   pl.BlockSpec((None, B_HEADS, B_HEAD_DIM, ts), lambda b, i: (b, 0, 0, i))],
        out_shape=[jax.ShapeDtypeStruct((Bn, S, B_WIDTH), BF16),
                   jax.ShapeDtypeStruct((Bn, B_HEADS, B_HEAD_DIM, S), F32)],
        compiler_params=_cparams(("arbitrary", "arbitrary")),
        name="mixer_b_prep",
    )(proj3, proj3, proj3, conv_w, conv_b, wq, wk)


def _split3(x):
    h1 = x.astype(BF16)
    r1 = x - h1.astype(F32)
    h2 = r1.astype(BF16)
    h3 = (r1 - h2.astype(F32)).astype(BF16)
    return h1, h2, h3


def _mlstm_kernel(qf_ref, ktf_ref, vf_ref, gf_ref, qb_ref, ktb_ref, vb_ref, gb_ref, bias_ref,
                  hf_ref, hb_ref, c_scr, m_scr, *, ts):
    L = MLSTM_CHUNK
    D = B_HEAD_DIM
    nch = ts // L

    @pl.when(pl.program_id(1) == 0)
    def _():
        c_scr[...] = jnp.zeros_like(c_scr)
        m_scr[...] = jnp.zeros_like(m_scr)

    row = lax.broadcasted_iota(jnp.int32, (L, L), 0)
    col = lax.broadcasted_iota(jnp.int32, (L, L), 1)
    bias = bias_ref[...]
    ones = jnp.ones((L, D), BF16)

    dirs = ((qf_ref, ktf_ref, vf_ref, gf_ref, hf_ref), (qb_ref, ktb_ref, vb_ref, gb_ref, hb_ref))
    masks = (col <= row, col >= row)
    lasts = (L - 1, 0)

    pre = {}
    items = [(d, cidx) for d in range(2) for cidx in range(nch)]
    tris = [jnp.where(masks[d], 1.0, 0.0).astype(BF16) for d in range(2)]
    gs, bsums, arows = {}, {}, {}
    for d, cidx in items:
        gs[d, cidx] = dirs[d][3][cidx * L:(cidx + 1) * L, :] + bias
    for d, cidx in items:
        g = gs[d, cidx]
        lf = -(jnp.maximum(-g, 0.0) + jnp.log1p(jnp.exp(-jnp.abs(g))))
        s1, s2, s3 = _split3(lf)
        tri = tris[d]
        bsums[d, cidx] = (jnp.dot(tri, s1, preferred_element_type=F32) + jnp.dot(tri, s2, preferred_element_type=F32)
                          + jnp.dot(tri, s3, preferred_element_type=F32))
    for d, cidx in items:
        bsums[d, cidx] = pltpu.roll(bsums[d, cidx], LANES - B_HEADS, 1)
    for d, cidx in items:
        arows[d, cidx] = (gs[d, cidx] - bsums[d, cidx]).T
    for d, cidx in items:
        for h in range(B_HEADS):
            j = 2 * B_HEADS * d + h
            a_row = arows[d, cidx][j:j + 1, :]
            a_max = jnp.max(jnp.where(masks[d], a_row, -jnp.inf), axis=1, keepdims=True)
            pre[d, cidx, h] = (a_row, bsums[d, cidx][:, j:j + 1], a_max)

    state = {(d, h): (c_scr[d, h], m_scr[d, h]) for d in range(2) for h in range(B_HEADS)}
    combos = [(d, h) for d in range(2) for h in range(B_HEADS)]
    for step in range(nch):
        work = {}
        for d, h in combos:
            q_ref, kt_ref, v_ref, _, _ = dirs[d]
            cidx = nch - 1 - step if d == 1 else step
            rows = slice(cidx * L, (cidx + 1) * L)
            sl = slice(h * D, (h + 1) * D)
            a_row, b_col, a_max = pre[d, cidx, h]
            caug, m_old = state[d, h]
            big_m = jnp.maximum(m_old, a_max)
            q, kt = q_ref[rows, sl], kt_ref[h, :, rows]
            vaug = jnp.concatenate([v_ref[rows, sl].astype(BF16), ones], axis=1)
            qk = jnp.dot(q, kt.astype(BF16), preferred_element_type=F32)
            inter = jnp.dot(q, caug.astype(BF16), preferred_element_type=F32)
            work[d, h] = (rows, sl, a_row, b_col, big_m, kt, vaug, qk, inter)
        for d, h in combos:
            rows, sl, a_row, b_col, big_m, kt, vaug, qk, inter = work[d, h]
            m_old = state[d, h][1]
            sc = qk * jnp.where(masks[d], jnp.exp(a_row - big_m), 0.0)
            na = (jnp.exp(m_old - big_m) * inter
                  + jnp.dot(sc.astype(BF16), vaug, preferred_element_type=F32))
            work[d, h] = (rows, sl, a_row, b_col, big_m, kt, vaug, na)
        for d, h in combos:
            rows, sl, a_row, b_col, big_m, kt, vaug, na = work[d, h]
            caug, m_old = state[d, h]
            last = lasts[d]
            m_last = big_m[last:last + 1, :]
            dirs[d][4][rows, sl] = na[:, :D] / jnp.maximum(jnp.abs(na[:, D:]), jnp.exp(-(b_col + big_m)))
            wkt = (kt * jnp.exp(a_row - m_last)).astype(BF16)
            state[d, h] = (jnp.exp(m_old - m_last) * caug + jnp.dot(wkt, vaug, preferred_element_type=F32),
                           b_col[last:last + 1, :] + m_last)
    for (d, h), (caug, m) in state.items():
        c_scr[d, h] = caug
        m_scr[d, h] = m


def mixer_b(qb, ktb, proj3, gate_bias, Bn, S, ts=512):
    ts = min(ts, S)
    nt = S // ts
    fwd = lambda blk: (lambda b, i: (b, i, blk))
    bwd = lambda blk: (lambda b, i: (b, nt - 1 - i, blk))

    def specs(idx):
        seq = idx(0)
        return [pl.BlockSpec((None, ts, B_WIDTH), seq),
                pl.BlockSpec((None, B_HEADS, B_HEAD_DIM, ts), lambda b, i: (b, 0, 0, seq(b, i)[1])),
                pl.BlockSpec((None, ts, B_WIDTH), idx(COL_BV // B_WIDTH)),
                pl.BlockSpec((None, ts, LANES), idx(COL_BG // LANES))]

    return pl.pallas_call(
        functools.partial(_mlstm_kernel, ts=ts),
        grid=(Bn, nt),
        in_specs=specs(fwd) + specs(bwd) + [pl.BlockSpec((1, LANES), lambda b, i: (0, 0))],
        out_specs=[pl.BlockSpec((None, ts, B_WIDTH), fwd(0)), pl.BlockSpec((None, ts, B_WIDTH), bwd(0))],
        out_shape=[jax.ShapeDtypeStruct((Bn, S, B_WIDTH), F32)] * 2,
        scratch_shapes=[pltpu.VMEM((2, B_HEADS, B_HEAD_DIM, 2 * B_HEAD_DIM), F32),
                        pltpu.VMEM((2, B_HEADS, 1, 1), F32)],
        compiler_params=_cparams(("arbitrary", "arbitrary")),
        name="mixer_b",
    )(qb, ktb, proj3, proj3, qb, ktb, proj3, proj3, gate_bias)


def _merge_kernel(x_ref, n1g_ref, wgate_ref, oa_ref, hf_ref, hb_ref, bo_ref, oc_ref, hng_ref,
                  wa_ref, wb_ref, wc_ref, wo_ref, y_ref):
    x = x_ref[...]
    ms = jnp.mean(x * x, axis=-1, keepdims=True)
    xn = ((x * lax.rsqrt(ms + NORM_EPS)) * n1g_ref[...]).astype(BF16)
    gates = [jax.nn.sigmoid(jnp.dot(xn, wgate_ref[:, j * D_MODEL:(j + 1) * D_MODEL], preferred_element_type=F32))
             for j in range(N_BRANCH)]
    y_a = jnp.dot(oa_ref[...], wa_ref[...], preferred_element_type=F32)
    hsum = hf_ref[...] + hb_ref[...]
    hng = hng_ref[...]
    bo = bo_ref[...]
    parts = []
    for h in range(B_HEADS):
        sl = slice(h * B_HEAD_DIM, (h + 1) * B_HEAD_DIM)
        hh = hsum[:, sl]
        ms = jnp.mean(hh * hh, axis=-1, keepdims=True)
        parts.append(jax.nn.sigmoid(bo[:, sl]) * ((hh * lax.rsqrt(ms + NORM_EPS)) * hng[:, sl]))
    o_b = jnp.concatenate(parts, axis=-1)
    y_b = jnp.dot(o_b.astype(BF16), wb_ref[...], preferred_element_type=F32)
    y_c = jnp.dot(oc_ref[...], wc_ref[...], preferred_element_type=F32)
    mixed = gates[0] * y_a + gates[1] * y_b + gates[2] * y_c
    y_ref[...] = x + jnp.dot(mixed.astype(BF16), wo_ref[...], preferred_element_type=F32)


def merge(x2d, n1g, wgate, proj, oa, hf, hb, oc, hng, wa, wb, wc, wo, tm=512):
    T = x2d.shape[0]
    row = lambda w, blk=0: pl.BlockSpec((tm, w), lambda i: (i, blk))
    return pl.pallas_call(
        _merge_kernel,
        grid=(T // tm,),
        in_specs=[row(D_MODEL), _resident(n1g), _resident(wgate), row(A_WIDTH),
                  row(B_WIDTH), row(B_WIDTH), row(B_WIDTH, COL_BO // B_WIDTH), row(C_WIDTH),
                  _resident(hng), _resident(wa), _resident(wb), _resident(wc), _resident(wo)],
        out_specs=row(D_MODEL),
        out_shape=jax.ShapeDtypeStruct((T, D_MODEL), F32),
        compiler_params=_cparams(("arbitrary",)),
        name="merge",
    )(x2d, n1g, wgate, oa, hf, hb, proj, oc, hng, wa, wb, wc, wo)


def _ffn_kernel(x_ref, g_ref, wg_ref, wu_ref, wd_ref, y_ref):
    x = x_ref[...]
    ms = jnp.mean(x * x, axis=-1, keepdims=True)
    xn = ((x * lax.rsqrt(ms + NORM_EPS)) * g_ref[...]).astype(BF16)
    hg = jnp.dot(xn, wg_ref[...], preferred_element_type=F32)
    hu = jnp.dot(xn, wu_ref[...], preferred_element_type=F32)
    a = ((hg * jax.nn.sigmoid(hg)) * hu).astype(BF16)
    y_ref[...] = x + jnp.dot(a, wd_ref[...], preferred_element_type=F32)


def ffn(x2d, g, wg, wu, wd, tm=512):
    T = x2d.shape[0]
    return pl.pallas_call(
        _ffn_kernel,
        grid=(T // tm,),
        in_specs=[pl.BlockSpec((tm, D_MODEL), lambda i: (i, 0)),
                  _resident(g), _resident(wg), _resident(wu), _resident(wd)],
        out_specs=pl.BlockSpec((tm, D_MODEL), lambda i: (i, 0)),
        out_shape=jax.ShapeDtypeStruct((T, D_MODEL), F32),
        compiler_params=_cparams(("arbitrary",)),
        name="ffn",
    )(x2d, g, wg, wu, wd)


def _rotary_tables(S):
    inv = ROPE_THETA ** (-jnp.arange(0, ROT_DIM, 2, dtype=F32) / ROT_DIM)
    ang = jnp.arange(S, dtype=F32)[:, None] * inv[None, :]
    cos, sin = jnp.cos(ang), jnp.sin(ang)
    half = ROT_DIM // 2
    rest = HEAD_DIM - ROT_DIM
    one, zero, z8 = jnp.ones((S, rest), F32), jnp.zeros((S, rest), F32), jnp.zeros((S, half), F32)
    c64 = jnp.concatenate([cos, cos, one], axis=1)
    lo64 = jnp.concatenate([-sin, z8, zero], axis=1)
    hi64 = jnp.concatenate([z8, sin, zero], axis=1)
    return tuple(jnp.tile(t, (1, LANES // HEAD_DIM)) for t in (c64, lo64, hi64))


def _pack_layer(norm1_g, w_in, a_qn_g, a_kn_g, b_conv_w, b_conv_b, b_wq, b_wk, b_gate_bias, b_hn_g,
                c_qn_g, c_kn_g, c_lambda, c_hn_g, w_branch, w_out, norm2_g, w_ffn_in, w_ffn_out, layer_idx):
    o_bc = 3 * A_QKV
    o_bg = o_bc + 3 * B_WIDTH
    o_cq = o_bg + B_GATES
    o_gate = o_cq + 3 * C_QK_WIDTH
    w_att = jnp.concatenate([w_in[:, :o_bc], w_in[:, o_cq:o_gate]], axis=1).astype(BF16)
    w_b = jnp.concatenate([w_in[:, o_bc:o_bg], w_in[:, o_bg:o_cq], jnp.zeros((D_MODEL, LANES - B_GATES), F32)],
                          axis=1).astype(BF16)
    lam_init = 0.8 - 0.6 * math.exp(-0.3 * layer_idx)
    tile2 = lambda g: jnp.tile(g.reshape(1, HEAD_DIM), (1, LANES // HEAD_DIM))
    return dict(
        norm1_g=norm1_g.reshape(1, D_MODEL), w_att=w_att, w_b=w_b, w_gate=w_in[:, o_gate:].astype(BF16),
        gains=(tile2(a_qn_g), tile2(a_kn_g), tile2(c_qn_g), tile2(c_kn_g)),
        conv_w=b_conv_w, conv_b=b_conv_b.reshape(1, B_WIDTH), wq=b_wq.astype(BF16),
        wkt=jnp.swapaxes(b_wk, 1, 2).astype(BF16),
        gate_bias=jnp.concatenate([b_gate_bias.reshape(1, B_GATES), jnp.zeros((1, LANES - B_GATES), F32)], axis=1),
        hn_g=b_hn_g.reshape(1, B_WIDTH),
        a_shift=_score_bound(a_qn_g, a_kn_g), c_shift=_score_bound(c_qn_g, c_kn_g),
        c_lambda=c_lambda, lam_init=jnp.full((1, 1), lam_init, F32),
        c_hn_g=(c_hn_g * (1.0 - lam_init)).reshape(1, C_V_DIM),
        wa=w_branch[:A_WIDTH].astype(BF16), wb=w_branch[A_WIDTH:A_WIDTH + B_WIDTH].astype(BF16),
        wc=w_branch[A_WIDTH + B_WIDTH:].astype(BF16), wo=w_out.astype(BF16),
        norm2_g=norm2_g.reshape(1, D_MODEL),
        wg=w_ffn_in[:, :D_FF].astype(BF16), wu=w_ffn_in[:, D_FF:].astype(BF16), wd=w_ffn_out.astype(BF16))


def _layer(x2d, p, tables, bd, Bn, S):
    proj = norm_matmul(x2d, p["norm1_g"], p["w_b"])
    *a_qkv, qc, kc, vc = attn_proj(x2d, p["norm1_g"], p["w_att"], tables, bd, p["gains"], Bn, S)
    oa = mixer_a(p["a_shift"], a_qkv, Bn, S)
    proj3 = proj.reshape(Bn, S, PROJ_COLS)
    qb, ktb = mixer_b_prep(proj3, p["conv_w"], p["conv_b"], p["wq"], p["wkt"], Bn, S)
    hf, hb = mixer_b(qb, ktb, proj3, p["gate_bias"], Bn, S)
    oc = mixer_c(qc, kc, vc, p["c_shift"], p["c_lambda"], p["lam_init"], p["c_hn_g"], Bn, S)
    x2d = merge(x2d, p["norm1_g"], p["w_gate"], proj, oa,
                hf.reshape(Bn * S, B_WIDTH), hb.reshape(Bn * S, B_WIDTH), oc,
                p["hn_g"], p["wa"], p["wb"], p["wc"], p["wo"])
    return ffn(x2d, p["norm2_g"], p["wg"], p["wu"], p["wd"])


def _trunk(x, layers, bd):
    Bn, S, _ = x.shape
    tables = _rotary_tables(S)
    x2d = x.reshape(Bn * S, D_MODEL)
    for p in layers:
        x2d = _layer(x2d, p, tables, bd, Bn, S)
    return x2d.reshape(Bn, S, D_MODEL)


def kernel(x_prompt, x_sample, norm1_g, w_in, a_qn_g, a_kn_g, b_conv_w, b_conv_b, b_wq, b_wk, b_gate_bias, b_hn_g,
           c_qn_g, c_kn_g, c_lambda, c_hn_g, w_branch, w_out, norm2_g, w_ffn_in, w_ffn_out):
    weights = (norm1_g, w_in, a_qn_g, a_kn_g, b_conv_w, b_conv_b, b_wq, b_wk, b_gate_bias, b_hn_g,
               c_qn_g, c_kn_g, c_lambda, c_hn_g, w_branch, w_out, norm2_g, w_ffn_in, w_ffn_out)
    depth = norm1_g.shape[0]
    layers = [_pack_layer(*[w[l] for w in weights], layer_idx=l) for l in range(depth)]
    lane = jnp.arange(2 * LANES) // HEAD_DIM
    bd = (lane[:, None] == lane[None, :]).astype(BF16)
    return (_trunk(x_prompt, layers, bd), _trunk(x_sample, layers, bd))
```
